```python
import jax, jax.numpy as jnp
from jax import lax
import numpy as np

D_MODEL = 2048
BATCH = 1
SEQ = 8192
DEPTH = 1
DEC_BATCH = 32
DEC_SEQ = 1
PAST_LEN = 16384
PAGE_SIZE = 128

M_HEADS = 4
M_HEAD_V = D_MODEL // 8
M_WIDTH = M_HEADS * M_HEAD_V
M_HEAD_QK = M_HEAD_V // 2
CONV_W = 4
M_CHUNK = 64
A_HEADS = 8
A_HEAD_DIM = D_MODEL // 16
A_WIDTH = A_HEADS * A_HEAD_DIM
IDX_HEADS = 16
IDX_DIM = 64
TOPK_MAX = 256
Q_BLOCK = 128
D_FF = ((8 * D_MODEL // 3 + 127) // 128) * 128
N_BRANCHES = 2
EPS = 1e-6
PROJ_SIZES = (M_WIDTH, M_WIDTH, M_WIDTH, M_HEADS, M_HEADS,
              A_WIDTH, A_WIDTH, A_WIDTH, IDX_HEADS * IDX_DIM, IDX_DIM, IDX_HEADS,
              N_BRANCHES * D_MODEL)
D_IN = sum(PROJ_SIZES)

kernel_name = 'hybrid_mlstm_dsa_macaron_step'


def rmsnorm(x, g):
    xf = x.astype(jnp.float32)
    y = xf * lax.rsqrt(jnp.mean(xf * xf, axis=-1, keepdims=True) + EPS)
    return (y * g.astype(jnp.float32)).astype(x.dtype)


def swiglu_half(x, g, w_up, w_down):
    a, b = jnp.split(rmsnorm(x, g) @ w_up, 2, axis=-1)
    return x + 0.5 * ((jax.nn.silu(a) * b) @ w_down)


def split_cols(p):
    return jnp.split(p, np.cumsum(PROJ_SIZES)[:-1].tolist(), axis=-1)


def causal_conv(u_ext, w, b):
    T = u_ext.shape[1] - (CONV_W - 1)
    return sum(u_ext[:, j:j + T] * w[j] for j in range(CONV_W)) + b


def mlstm_chunk(carry, xs):
    C, n, m = carry
    q, k, v, log_i, log_f = xs
    L = q.shape[1]
    b = jnp.moveaxis(jnp.cumsum(log_f, axis=1), 1, 2)
    li = jnp.moveaxis(log_i, 1, 2)
    causal = jnp.tril(jnp.ones((L, L), dtype=bool))
    dmat = jnp.where(causal, b[..., :, None] - b[..., None, :] + li[..., None, :], -jnp.inf)
    inter = b + m[..., None]
    m_t = jnp.maximum(inter, jnp.max(dmat, axis=-1))
    s_inter = jnp.exp(inter - m_t)
    S = jnp.einsum('blhk,bshk->bhls', q, k) * jnp.exp(dmat - m_t[..., None])
    num = (jnp.einsum('bhls,bshv->blhv', S, v)
           + jnp.einsum('bhl,blhk,bhkv->blhv', s_inter, q, C))
    den = jnp.sum(S, axis=-1) + s_inter * jnp.einsum('blhk,bhk->bhl', q, n)
    denom = jnp.maximum(jnp.abs(den), jnp.exp(-m_t))
    h = num / jnp.moveaxis(denom, 1, 2)[..., None]
    bL = b[..., -1]
    g = bL[..., None] - b + li
    m_new = jnp.maximum(bL + m, jnp.max(g, axis=-1))
    s_old = jnp.exp(bL + m - m_new)
    wk = jnp.exp(g - m_new[..., None])
    C_new = s_old[..., None, None] * C + jnp.einsum('bhs,bshk,bshv->bhkv', wk, k, v)
    n_new = s_old[..., None] * n + jnp.einsum('bhs,bshk->bhk', wk, k)
    return (C_new, n_new, m_new), h


def index_scores(qi, wi, ki):
    s = jax.nn.relu(jnp.einsum('bthd,bsd->bths', qi, ki).astype(jnp.float32))
    return jnp.einsum('bths,bth->bts', s, wi.astype(jnp.float32))


def sparse_attend(q, k_sel, v_sel, valid):
    s = jnp.einsum('bthd,btkhd->bthk', q, k_sel).astype(jnp.float32) * (A_HEAD_DIM ** -0.5)
    s = jnp.where(valid[:, :, None, :], s, -jnp.inf)
    p = jax.nn.softmax(s, axis=-1)
    return jnp.einsum('bthk,btkhd->bthd', p.astype(v_sel.dtype), v_sel)


def dsa_prompt(q, k, v, qi, ki, wi):
    B, T = q.shape[:2]
    K = min(TOPK_MAX, T // 4)
    nb = T // Q_BLOCK
    key_pos = jnp.arange(T)
    gather = jax.vmap(lambda rows, idx: rows[idx])

    def block(args):
        bi, qb, qib, wib = args
        pos = bi * Q_BLOCK + jnp.arange(Q_BLOCK)
        sc = index_scores(qib, wib, ki)
        sc = jnp.where((key_pos[None, :] <= pos[:, None])[None], sc, -jnp.inf)
        _, idx = lax.top_k(sc, K)
        valid = idx <= pos[None, :, None]
        return sparse_attend(qb, gather(k, idx), gather(v, idx), valid)

    to_blocks = lambda a: jnp.moveaxis(a.reshape(B, nb, Q_BLOCK, *a.shape[2:]), 1, 0)
    out = lax.map(block, (jnp.arange(nb), to_blocks(q), to_blocks(qi), to_blocks(wi)))
    return jnp.moveaxis(out, 0, 1).reshape(B, T, A_HEADS, A_HEAD_DIM)


def dsa_sample(q, k_new, v_new, qi, ki_new, wi, l, cache_k, cache_v, cache_kidx, page_table):
    DB, T = q.shape[:2]
    past = page_table.shape[1] * PAGE_SIZE
    L = past + T
    K = min(TOPK_MAX, L // 4)
    ki_past = cache_kidx[l, page_table].reshape(DB, past, IDX_DIM)
    ki_all = jnp.concatenate([ki_past.astype(ki_new.dtype), ki_new], axis=1)
    sc = index_scores(qi, wi, ki_all)
    pos = past + jnp.arange(T)
    sc = jnp.where((jnp.arange(L)[None, :] <= pos[:, None])[None], sc, -jnp.inf)
    _, idx = lax.top_k(sc, K)
    valid = idx <= pos[None, :, None]
    in_past = (idx < past)[..., None, None]
    pidx = jnp.minimum(idx, past - 1)
    phys = page_table[jnp.arange(DB)[:, None, None], pidx // PAGE_SIZE]
    off = pidx % PAGE_SIZE
    nidx = jnp.clip(idx - past, 0, T - 1)
    gather = jax.vmap(lambda rows, ii: rows[ii])
    k_sel = jnp.where(in_past, cache_k[l, phys, off].astype(k_new.dtype), gather(k_new, nidx))
    v_sel = jnp.where(in_past, cache_v[l, phys, off].astype(v_new.dtype), gather(v_new, nidx))
    return sparse_attend(q, k_sel, v_sel, valid)


def token_mixing(x, conv_prev, C0, n0, m0, attend, g_mix, w_in, b_if, w_conv, b_conv,
                 w_qk_m, g_mout, g_q, g_k, w_mo, w_ao, w_out):
    B, T, _ = x.shape
    h = rmsnorm(x, g_mix)
    (u_m, v_m, o_m, i_m, f_m, q_a, k_a, v_a, q_i, k_i, w_i, gate) = split_cols(h @ w_in)
    u_ext = jnp.concatenate([conv_prev.astype(x.dtype), u_m], axis=1)
    c = jax.nn.silu(causal_conv(u_ext, w_conv, b_conv)).reshape(B, T, M_HEADS, M_HEAD_V)
    q_m, k_m = jnp.split(jnp.einsum('bthc,hcd->bthd', c, w_qk_m), 2, axis=-1)
    q_m = q_m * (M_HEAD_QK ** -0.5)
    gates = jnp.concatenate([i_m, f_m], axis=-1).astype(jnp.float32) + b_if.astype(jnp.float32)
    log_i = gates[..., :M_HEADS]
    log_f = jax.nn.log_sigmoid(gates[..., M_HEADS:])
    chunk = M_CHUNK if T % M_CHUNK == 0 else T
    nc = T // chunk
    to_chunks = lambda a: jnp.moveaxis(a.reshape(B, nc, chunk, *a.shape[2:]), 1, 0)
    xs = (to_chunks(q_m), to_chunks(k_m), to_chunks(v_m.reshape(B, T, M_HEADS, M_HEAD_V)),
          to_chunks(log_i), to_chunks(log_f))
    carry0 = (C0.astype(jnp.float32), n0.astype(jnp.float32), m0.astype(jnp.float32))
    (C1, n1, m1), hs = lax.scan(mlstm_chunk, carry0, xs)
    h_m = jnp.moveaxis(hs, 0, 1).reshape(B, T, M_HEADS, M_HEAD_V).astype(x.dtype)
    h_m = rmsnorm(h_m, g_mout).reshape(B, T, M_WIDTH) * jax.nn.sigmoid(o_m)
    q = rmsnorm(q_a.reshape(B, T, A_HEADS, A_HEAD_DIM), g_q)
    k = rmsnorm(k_a.reshape(B, T, A_HEADS, A_HEAD_DIM), g_k)
    v = v_a.reshape(B, T, A_HEADS, A_HEAD_DIM)
    qi = q_i.reshape(B, T, IDX_HEADS, IDX_DIM)
    wi = w_i * ((IDX_HEADS ** -0.5) * (IDX_DIM ** -0.5))
    h_a = attend(q, k, v, qi, k_i, wi).reshape(B, T, A_WIDTH)
    g = jax.nn.sigmoid(gate.astype(jnp.float32)).astype(x.dtype)
    g_m, g_a = jnp.split(g, 2, axis=-1)
    y = x + (g_m * (h_m @ w_mo) + g_a * (h_a @ w_ao)) @ w_out
    return y, k, v, k_i, u_ext[:, -(CONV_W - 1):], C1, n1, m1


def setup_inputs(seed: int = 0) -> dict:
    key = jax.random.key(seed)
    ks = iter(jax.random.split(key, 40))
    nrm = lambda shape, scale: jax.random.normal(next(ks), shape, jnp.float32) * scale
    gain = lambda shape: 1.0 + nrm(shape, 0.02)
    n_pages = PAST_LEN // PAGE_SIZE
    in_use = DEC_BATCH * n_pages
    n_pool = in_use + max(1, in_use // 4)
    return {
        'x_prompt': nrm((BATCH, SEQ, D_MODEL), 1.0),
        'x_sample': nrm((DEC_BATCH, DEC_SEQ, D_MODEL), 1.0),
        'cache_k': nrm((DEPTH, n_pool, PAGE_SIZE, A_HEADS, A_HEAD_DIM), 1.0),
        'cache_v': nrm((DEPTH, n_pool, PAGE_SIZE, A_HEADS, A_HEAD_DIM), 1.0),
        'cache_kidx': nrm((DEPTH, n_pool, PAGE_SIZE, IDX_DIM), 1.0),
        'state_conv': nrm((DEPTH, DEC_BATCH, CONV_W - 1, M_WIDTH), 1.0),
        'state_C': nrm((DEPTH, DEC_BATCH, M_HEADS, M_HEAD_QK, M_HEAD_V), 1.0),
        'state_n': nrm((DEPTH, DEC_BATCH, M_HEADS, M_HEAD_QK), 1.0),
        'state_m': nrm((DEPTH, DEC_BATCH, M_HEADS), 0.5),
        'page_table': jax.random.permutation(next(ks), n_pool)[:in_use].reshape(DEC_BATCH, n_pages).astype(jnp.int32),
        'norm_ffn1': gain((DEPTH, D_MODEL)),
        'w_ffn1_up': nrm((DEPTH, D_MODEL, 2 * D_FF), D_MODEL ** -0.5),
        'w_ffn1_down': nrm((DEPTH, D_FF, D_MODEL), D_FF ** -0.5),
        'norm_mix': gain((DEPTH, D_MODEL)),
        'w_in': nrm((DEPTH, D_MODEL, D_IN), D_MODEL ** -0.5),
        'b_if': jnp.concatenate([nrm((DEPTH, M_HEADS), 0.01), 3.0 + nrm((DEPTH, M_HEADS), 0.1)], axis=-1),
        'w_conv': nrm((DEPTH, CONV_W, M_WIDTH), CONV_W ** -0.5),
        'b_conv': nrm((DEPTH, M_WIDTH), 0.01),
        'w_qk_m': nrm((DEPTH, M_HEADS, M_HEAD_V, 2 * M_HEAD_QK), M_HEAD_V ** -0.5),
        'norm_mlstm_out': gain((DEPTH, M_HEADS, M_HEAD_V)),
        'norm_q': gain((DEPTH, A_HEAD_DIM)),
        'norm_k': gain((DEPTH, A_HEAD_DIM)),
        'w_mo': nrm((DEPTH, M_WIDTH, D_MODEL), M_WIDTH ** -0.5),
        'w_ao': nrm((DEPTH, A_WIDTH, D_MODEL), A_WIDTH ** -0.5),
        'w_out': nrm((DEPTH, D_MODEL, D_MODEL), D_MODEL ** -0.5),
        'norm_ffn2': gain((DEPTH, D_MODEL)),
        'w_ffn2_up': nrm((DEPTH, D_MODEL, 2 * D_FF), D_MODEL ** -0.5),
        'w_ffn2_down': nrm((DEPTH, D_FF, D_MODEL), D_FF ** -0.5),
    }


def reference(x_prompt, x_sample, cache_k, cache_v, cache_kidx, state_conv, state_C, state_n, state_m,
              page_table, norm_ffn1, w_ffn1_up, w_ffn1_down, norm_mix, w_in, b_if, w_conv, b_conv,
              w_qk_m, norm_mlstm_out, norm_q, norm_k, w_mo, w_ao, w_out, norm_ffn2, w_ffn2_up, w_ffn2_down):
    xp, xs = x_prompt, x_sample
    out_p = [[] for _ in range(7)]
    out_s = [[] for _ in range(7)]
    for l in range(DEPTH):
        wl = (norm_mix[l], w_in[l], b_if[l], w_conv[l], b_conv[l], w_qk_m[l], norm_mlstm_out[l],
              norm_q[l], norm_k[l], w_mo[l], w_ao[l], w_out[l])
        xp = swiglu_half(xp, norm_ffn1[l], w_ffn1_up[l], w_ffn1_down[l])
        xs = swiglu_half(xs, norm_ffn1[l], w_ffn1_up[l], w_ffn1_down[l])
        B = xp.shape[0]
        xp, *st_p = token_mixing(
            xp, jnp.zeros((B, CONV_W - 1, M_WIDTH), xp.dtype),
            jnp.zeros((B, M_HEADS, M_HEAD_QK, M_HEAD_V), jnp.float32),
            jnp.zeros((B, M_HEADS, M_HEAD_QK), jnp.float32),
            jnp.zeros((B, M_HEADS), jnp.float32),
            dsa_prompt, *wl)
        attend_s = lambda q, k, v, qi, ki, wi, l=l: dsa_sample(
            q, k, v, qi, ki, wi, l, cache_k, cache_v, cache_kidx, page_table)
        xs, *st_s = token_mixing(xs, state_conv[l], state_C[l], state_n[l], state_m[l], attend_s, *wl)
        xp = swiglu_half(xp, norm_ffn2[l], w_ffn2_up[l], w_ffn2_down[l])
        xs = swiglu_half(xs, norm_ffn2[l], w_ffn2_up[l], w_ffn2_down[l])
        for j in range(7):
            out_p[j].append(st_p[j])
            out_s[j].append(st_s[j])
    sp = [jnp.stack(a, axis=0) for a in out_p]
    ss = [jnp.stack(a, axis=0) for a in out_s]
    return (xp, xs, sp[0], sp[1], sp[2], sp[3], sp[4], sp[5], sp[6],
            ss[0], ss[1], ss[2], ss[3], ss[4], ss[5], ss[6])
```

```python
import functools
import math

import jax
import jax.numpy as jnp
from jax import lax
from jax.experimental import pallas as pl
from jax.experimental.pallas import tpu as pltpu

F32 = jnp.float32
BF16 = jnp.bfloat16
I32 = jnp.int32

EPS = 1e-6
LANES = 128
PAGE_SIZE = 128
M_HEADS = 4
M_CHUNK = 64
CONV_W = 4
A_HEADS = 8
IDX_HEADS = 16
IDX_DIM = 64
TOPK_MAX = 256
N_BRANCHES = 2
GROUP = 1024
INT_MIN = -2 ** 31
NEG_INIT = -1e30
LOG2E = 1.4426950408889634
VMEM_LIMIT = 56 * 1024 * 1024

S_KIDX = 0
S_WIDX = IDX_DIM
S_IGATE = IDX_DIM + IDX_HEADS
S_FGATE = S_IGATE + M_HEADS

C_GATE, C_UM, C_VM, C_OM, C_QA, C_KA, C_VA, C_QI = 0, 4, 5, 6, 7, 8, 9, 10
N_BIG_BLOCKS = 11


def _cparams(sem, vmem=VMEM_LIMIT):
    return pltpu.CompilerParams(dimension_semantics=sem, vmem_limit_bytes=vmem)


def _vmem_spec():
    return pl.BlockSpec(memory_space=pltpu.VMEM)


def _rms(x, g):
    return x * lax.rsqrt(jnp.mean(x * x, axis=-1, keepdims=True) + EPS) * g


def _sigmoid(x):
    return 1.0 / (1.0 + jnp.exp(-x))


def _log_sigmoid(x):
    return jnp.minimum(x, 0.0) - jnp.log(1.0 + jnp.exp(-jnp.abs(x)))


def _dot(a, b):
    return jnp.dot(a, b, preferred_element_type=F32)


def _dot_nt(a, b):
    return lax.dot_general(a, b, (((1,), (1,)), ((), ())), preferred_element_type=F32)


def _dot_exact(a, b):
    return jnp.dot(a, b, preferred_element_type=F32, precision=lax.Precision.HIGHEST)


def _mono_key(x):
    b = lax.bitcast_convert_type(x, I32)
    return b ^ ((b >> 31) & jnp.int32(0x7FFFFFFF))


def _ffn_kernel(x_ref, g_ref, wu_ref, wd_ref, o_ref, h_ref, *, tf, nf):
    j = pl.program_id(1)

    @pl.when(j == 0)
    def _():
        h_ref[...] = _rms(x_ref[...], g_ref[...]).astype(BF16)

    ab = _dot(h_ref[...], wu_ref[...])
    a = ab[:, :tf]
    act = (a * _sigmoid(a) * ab[:, tf:]).astype(BF16)
    d = _dot(act, wd_ref[...])

    @pl.when(j == 0)
    def _():
        o_ref[...] = d

    @pl.when(j > 0)
    def _():
        o_ref[...] += d

    @pl.when(j == nf - 1)
    def _():
        o_ref[...] = x_ref[...] + 0.5 * o_ref[...]


def _ffn(x, g, wu, wd, *, tm, tf):
    m, d = x.shape
    nf = wd.shape[0] // tf
    return pl.pallas_call(
        functools.partial(_ffn_kernel, tf=tf, nf=nf),
        out_shape=jax.ShapeDtypeStruct((m, d), F32),
        grid=(pl.cdiv(m, tm), nf),
        in_specs=[
            pl.BlockSpec((tm, d), lambda i, j: (i, 0)),
            pl.BlockSpec((1, d), lambda i, j: (0, 0)),
            pl.BlockSpec((d, 2 * tf), lambda i, j: (0, j)),
            pl.BlockSpec((tf, d), lambda i, j: (j, 0)),
        ],
        out_specs=pl.BlockSpec((tm, d), lambda i, j: (i, 0)),
        scratch_shapes=[pltpu.VMEM((tm, d), BF16)],
        compiler_params=_cparams(("arbitrary", "arbitrary")),
        name="ffn",
    )(x, g, wu, wd)


def _prep_ffn_weights(w_up, w_down, tf):
    d, two_ff = w_up.shape
    d_ff = two_ff // 2
    d_pad = pl.cdiv(d_ff, tf) * tf
    nf = d_pad // tf
    pad = ((0, 0), (0, d_pad - d_ff))
    a = jnp.pad(w_up[:, :d_ff], pad).reshape(d, nf, 1, tf)
    b = jnp.pad(w_up[:, d_ff:], pad).reshape(d, nf, 1, tf)
    wu = jnp.concatenate([a, b], axis=2).reshape(d, nf * 2 * tf).astype(BF16)
    wd = jnp.pad(w_down, ((0, d_pad - d_ff), (0, 0))).astype(BF16)
    return wu, wd


def _inproj_kernel(x_ref, g_ref, w_ref, ws_ref, p_ref, ps_ref, h_ref):
    @pl.when(pl.program_id(1) == 0)
    def _():
        h = _rms(x_ref[...], g_ref[...]).astype(BF16)
        h_ref[...] = h
        ps_ref[...] = _dot(h, ws_ref[...])

    p_ref[...] = _dot(h_ref[...], w_ref[...])


def _inproj(x, g, w_big, w_small, *, tm):
    m, d = x.shape
    n = w_big.shape[1]
    return pl.pallas_call(
        _inproj_kernel,
        out_shape=(jax.ShapeDtypeStruct((m, n), F32), jax.ShapeDtypeStruct((m, LANES), F32)),
        grid=(pl.cdiv(m, tm), n // GROUP),
        in_specs=[
            pl.BlockSpec((tm, d), lambda i, j: (i, 0)),
            pl.BlockSpec((1, d), lambda i, j: (0, 0)),
            pl.BlockSpec((d, GROUP), lambda i, j: (0, j)),
            pl.BlockSpec((d, LANES), lambda i, j: (0, 0)),
        ],
        out_specs=(pl.BlockSpec((tm, GROUP), lambda i, j: (i, j)),
                   pl.BlockSpec((tm, LANES), lambda i, j: (i, 0))),
        scratch_shapes=[pltpu.VMEM((tm, d), BF16)],
        compiler_params=_cparams(("arbitrary", "arbitrary")),
        name="inproj",
    )(x, g, w_big, w_small)


def _prep_w_in(w_in, d_model):
    mw = GROUP
    sizes = (mw, mw, mw, M_HEADS, M_HEADS, GROUP, GROUP, GROUP, IDX_HEADS * IDX_DIM, IDX_DIM,
             IDX_HEADS, N_BRANCHES * d_model)
    offs = [0]
    for s in sizes:
        offs.append(offs[-1] + s)
    col = lambda k: w_in[:, offs[k]:offs[k + 1]]
    u_m, v_m, o_m, i_m, f_m, q_a, k_a, v_a, q_i, k_i, w_i, gate = (col(k) for k in range(12))
    w_big = jnp.concatenate([gate, u_m, v_m, o_m, q_a, k_a, v_a, q_i], axis=1).astype(BF16)
    fill = jnp.zeros((w_in.shape[0], LANES - S_FGATE - M_HEADS), w_in.dtype)
    w_small = jnp.concatenate([k_i, w_i, i_m, f_m, fill], axis=1).astype(BF16)
    return w_big, w_small


def _mlstm_prompt_kernel(u_ref, v_ref, o_ref, ps_ref, bif_ref, wconv_ref, bconv_ref, wqk_ref, gmo_ref,
                         hm_ref, conv_ref, cst_ref, nst_ref, mst_ref,
                         ext_ref, c_ref, n_ref, m_ref, *, rows, nsteps):
    i = pl.program_id(0)
    mw = u_ref.shape[1]
    hv = mw // M_HEADS
    hk = hv // 2
    L = M_CHUNK

    @pl.when(i == 0)
    def _():
        ext_ref[0:8, :] = jnp.zeros((8, mw), F32)
        c_ref[...] = jnp.zeros(c_ref.shape, F32)
        n_ref[...] = jnp.zeros(n_ref.shape, F32)
        m_ref[...] = jnp.zeros(m_ref.shape, F32)

    u = u_ref[...]
    ext_ref[8:8 + rows, :] = u
    wc = wconv_ref[...]
    cv = (ext_ref[5:5 + rows, :] * wc[0:1] + ext_ref[6:6 + rows, :] * wc[1:2]
          + ext_ref[7:7 + rows, :] * wc[2:3] + u * wc[3:4]) + bconv_ref[...]
    ext_ref[0:8, :] = u[rows - 8:rows, :]
    cb = (cv * _sigmoid(cv)).astype(BF16)

    g = ps_ref[...] + bif_ref[...]
    lf = _log_sigmoid(g)
    r_i = lax.broadcasted_iota(I32, (rows, rows), 0)
    c_i = lax.broadcasted_iota(I32, (rows, rows), 1)
    tri = jnp.where((r_i // L == c_i // L) & (c_i <= r_i), 1.0, 0.0).astype(F32)
    bcum = _dot_exact(tri, lf)
    g_t = g.T
    b_t = bcum.T
    causal = lax.broadcasted_iota(I32, (L, L), 1) <= lax.broadcasted_iota(I32, (L, L), 0)

    for h in range(M_HEADS):
        hs = slice(h * hv, (h + 1) * hv)
        qk = _dot(cb[:, hs], wqk_ref[h])
        q = qk[:, :hk] * (hk ** -0.5)
        k = qk[:, hk:]
        v = v_ref[:, hs]
        li_col = g[:, S_IGATE + h:S_IGATE + h + 1]
        b_col = bcum[:, S_FGATE + h:S_FGATE + h + 1]
        li_row = g_t[S_IGATE + h:S_IGATE + h + 1, :]
        b_row = b_t[S_FGATE + h:S_FGATE + h + 1, :]
        cmat = c_ref[h]
        nvec = n_ref[h:h + 1, :]
        m = m_ref[h:h + 1, 0:1]
        for c in range(rows // L):
            sl = slice(c * L, (c + 1) * L)
            qc, kc, vc = q[sl], k[sl], v[sl]
            bc, lic = b_col[sl], li_col[sl]
            br, lir = b_row[:, sl], li_row[:, sl]
            qcb, kcb, vcb = qc.astype(BF16), kc.astype(BF16), vc.astype(BF16)
            dmat = jnp.where(causal, (bc - br) + lir, -jnp.inf)
            inter = bc + m
            m_t = jnp.maximum(inter, jnp.max(dmat, axis=-1, keepdims=True))
            s_inter = jnp.exp(inter - m_t)
            s = _dot_nt(qcb, kcb) * jnp.exp(dmat - m_t)
            num = _dot(s.astype(BF16), vcb) + s_inter * _dot(qcb, cmat.astype(BF16))
            den = (jnp.sum(s, axis=-1, keepdims=True)
                   + s_inter * jnp.sum(qc * nvec, axis=-1, keepdims=True))
            hc = num / jnp.maximum(jnp.abs(den), jnp.exp(-m_t))
            hn = _rms(hc, gmo_ref[h:h + 1, :]) * _sigmoid(o_ref[sl, hs])
            hm_ref[sl, hs] = hn.astype(BF16)
            b_last = bc[L - 1:L, :]
            g_row = (b_last - br) + lir
            g_col = (b_last - bc) + lic
            m_new = jnp.maximum(b_last + m, jnp.max(g_row, axis=-1, keepdims=True))
            s_old = jnp.exp(b_last + m - m_new)
            kw = kc * jnp.exp(g_col - m_new)
            cmat = s_old * cmat + _dot(kw.T.astype(BF16), vcb)
            nvec = s_old * nvec + jnp.sum(kw, axis=0, keepdims=True)
            m = m_new
        c_ref[h] = cmat
        n_ref[h:h + 1, :] = nvec
        m_ref[h:h + 1, :] = jnp.broadcast_to(m, (1, LANES))

    @pl.when(i == nsteps - 1)
    def _():
        conv_ref[...] = ext_ref[8 + rows - (CONV_W - 1):8 + rows, :]
        cst_ref[...] = c_ref[...]
        nst_ref[...] = n_ref[...]
        mst_ref[...] = m_ref[...]


def _mlstm_prompt(p, ps, bif_row, w_conv, b_conv, wqk, gmo, *, rows):
    t = p.shape[0]
    mw = GROUP
    hv = mw // M_HEADS
    hk = hv // 2
    nsteps = t // rows
    const = lambda shape: pl.BlockSpec(shape, lambda i: (0,) * len(shape))
    return pl.pallas_call(
        functools.partial(_mlstm_prompt_kernel, rows=rows, nsteps=nsteps),
        out_shape=(jax.ShapeDtypeStruct((t, mw), BF16),
                   jax.ShapeDtypeStruct((CONV_W - 1, mw), F32),
                   jax.ShapeDtypeStruct((M_HEADS, hk, hv), F32),
                   jax.ShapeDtypeStruct((8, hk), F32),
                   jax.ShapeDtypeStruct((8, LANES), F32)),
        grid=(nsteps,),
        in_specs=[
            pl.BlockSpec((rows, mw), lambda i: (i, C_UM)),
            pl.BlockSpec((rows, mw), lambda i: (i, C_VM)),
            pl.BlockSpec((rows, mw), lambda i: (i, C_OM)),
            pl.BlockSpec((rows, LANES), lambda i: (i, 0)),
            const((1, LANES)), const((CONV_W, mw)), const((1, mw)),
            const((M_HEADS, hv, 2 * hk)), const((M_HEADS, hv)),
        ],
        out_specs=(pl.BlockSpec((rows, mw), lambda i: (i, 0)),
                   const((CONV_W - 1, mw)), const((M_HEADS, hk, hv)), const((8, hk)), const((8, LANES))),
        scratch_shapes=[pltpu.VMEM((rows + 8, mw), F32), pltpu.VMEM((M_HEADS, hk, hv), F32),
                        pltpu.VMEM((8, hk), F32), pltpu.VMEM((8, LANES), F32)],
        compiler_params=_cparams(("arbitrary",)),
        name="mlstm_prompt",
    )(p, p, p, ps, bif_row, w_conv, b_conv, wqk, gmo)


def _mlstm_sample_kernel(u_ref, v_ref, o_ref, ps_ref, sc_ref, c0_ref, n0_ref, m0_ref,
                         bif_ref, wconv_ref, bconv_ref, wqk_ref, gmo_ref,
                         hm_ref, conv_ref, c1_ref, n1_ref, m1_ref, *, gb):
    mw = u_ref.shape[1]
    hv = mw // M_HEADS
    hk = hv // 2
    u = u_ref[...]
    wc = wconv_ref[...]
    cv = (sc_ref[0] * wc[0:1] + sc_ref[1] * wc[1:2] + sc_ref[2] * wc[2:3] + u * wc[3:4]) + bconv_ref[...]
    conv_ref[0] = sc_ref[1]
    conv_ref[1] = sc_ref[2]
    conv_ref[2] = u
    cb = (cv * _sigmoid(cv)).astype(BF16)
    g = ps_ref[...] + bif_ref[...]
    lf = _log_sigmoid(g)
    for h in range(M_HEADS):
        hs = slice(h * hv, (h + 1) * hv)
        qk = _dot(cb[:, hs], wqk_ref[h])
        q = qk[:, :hk] * (hk ** -0.5)
        k = qk[:, hk:]
        fill = jnp.zeros((LANES - gb, hk), F32)
        q_t = jnp.concatenate([q, fill], axis=0).T
        k_t = jnp.concatenate([k, fill], axis=0).T
        v = v_ref[:, hs]
        for s in range(gb):
            li = g[s:s + 1, S_IGATE + h:S_IGATE + h + 1]
            b = lf[s:s + 1, S_FGATE + h:S_FGATE + h + 1]
            m0 = m0_ref[s:s + 1, h:h + 1]
            qr, kr, vr = q[s:s + 1], k[s:s + 1], v[s:s + 1]
            cmat = c0_ref[s, h]
            nrow = n0_ref[s, h:h + 1, :]
            inter = b + m0
            m_t = jnp.maximum(inter, li)
            s_inter = jnp.exp(inter - m_t)
            dw = jnp.exp(li - m_t)
            sqk = jnp.sum(qr * kr, axis=-1, keepdims=True) * dw
            qc = jnp.sum(q_t[:, s:s + 1] * cmat, axis=0, keepdims=True)
            num = sqk * vr + s_inter * qc
            den = sqk + s_inter * jnp.sum(qr * nrow, axis=-1, keepdims=True)
            hc = num / jnp.maximum(jnp.abs(den), jnp.exp(-m_t))
            hn = _rms(hc, gmo_ref[h:h + 1, :]) * _sigmoid(o_ref[s:s + 1, hs])
            hm_ref[s:s + 1, hs] = hn.astype(BF16)
            c1_ref[s, h] = s_inter * cmat + (dw * k_t[:, s:s + 1]) * vr
            n1_ref[s, h:h + 1, :] = s_inter * nrow + dw * kr
            m1_ref[s:s + 1, h:h + 1] = m_t


def _mlstm_sample(p, ps, conv_t, c0, n0, m0, bif_row, w_conv, b_conv, wqk, gmo, *, gb):
    db = p.shape[0]
    mw = GROUP
    hv = mw // M_HEADS
    hk = hv // 2
    const = lambda shape: pl.BlockSpec(shape, lambda i: (0,) * len(shape))
    return pl.pallas_call(
        functools.partial(_mlstm_sample_kernel, gb=gb),
        out_shape=(jax.ShapeDtypeStruct((db, mw), BF16),
                   jax.ShapeDtypeStruct((CONV_W - 1, db, mw), F32),
                   jax.ShapeDtypeStruct((db, M_HEADS, hk, hv), F32),
                   jax.ShapeDtypeStruct((db, M_HEADS, hk), F32),
                   jax.ShapeDtypeStruct((db, M_HEADS), F32)),
        grid=(db // gb,),
        in_specs=[
            pl.BlockSpec((gb, mw), lambda i: (i, C_UM)),
            pl.BlockSpec((gb, mw), lambda i: (i, C_VM)),
            pl.BlockSpec((gb, mw), lambda i: (i, C_OM)),
            pl.BlockSpec((gb, LANES), lambda i: (i, 0)),
            pl.BlockSpec((CONV_W - 1, gb, mw), lambda i: (0, i, 0)),
            pl.BlockSpec((gb, M_HEADS, hk, hv), lambda i: (i, 0, 0, 0)),
            pl.BlockSpec((gb, M_HEADS, hk), lambda i: (i, 0, 0)),
            pl.BlockSpec((gb, M_HEADS), lambda i: (i, 0)),
            const((1, LANES)), const((CONV_W, mw)), const((1, mw)),
            const((M_HEADS, hv, 2 * hk)), const((M_HEADS, hv)),
        ],
        out_specs=(pl.BlockSpec((gb, mw), lambda i: (i, 0)),
                   pl.BlockSpec((CONV_W - 1, gb, mw), lambda i: (0, i, 0)),
                   pl.BlockSpec((gb, M_HEADS, hk, hv), lambda i: (i, 0, 0, 0)),
                   pl.BlockSpec((gb, M_HEADS, hk), lambda i: (i, 0, 0)),
                   pl.BlockSpec((gb, M_HEADS), lambda i: (i, 0))),
        compiler_params=_cparams(("arbitrary",)),
        name="mlstm_sample",
    )(p, p, p, ps, conv_t, c0, n0, m0, bif_row, w_conv, b_conv, wqk, gmo)


def _dsa_prep_kernel(k_ref, v_ref, ps_ref, gk_ref, k32_ref, v32_ref, kb_ref, vb_ref, kie_ref, kio_ref):
    hd = gk_ref.shape[1]
    for h in range(A_HEADS):
        hs = slice(h * hd, (h + 1) * hd)
        kn = _rms(k_ref[:, hs], gk_ref[...])
        k32_ref[:, hs] = kn
        kb_ref[:, hs] = kn.astype(BF16)
    v = v_ref[...]
    v32_ref[...] = v
    vb_ref[...] = v.astype(BF16)
    ps = ps_ref[...]
    lane = lax.broadcasted_iota(I32, ps.shape, 1)
    ke = jnp.where(lane < IDX_DIM, ps, 0.0)
    kie_ref[...] = ke.astype(BF16)
    kio_ref[...] = pltpu.roll(ke, IDX_DIM, axis=1).astype(BF16)


def _dsa_prep(p, ps, gk, *, tm):
    t = p.shape[0]
    hd = gk.shape[1]
    row = lambda w, c: pl.BlockSpec((tm, w), lambda i: (i, c))
    return pl.pallas_call(
        _dsa_prep_kernel,
        out_shape=(jax.ShapeDtypeStruct((t, GROUP), F32), jax.ShapeDtypeStruct((t, GROUP), F32),
                   jax.ShapeDtypeStruct((t, GROUP), BF16), jax.ShapeDtypeStruct((t, GROUP), BF16),
                   jax.ShapeDtypeStruct((t, LANES), BF16), jax.ShapeDtypeStruct((t, LANES), BF16)),
        grid=(pl.cdiv(t, tm),),
        in_specs=[row(GROUP, C_KA), row(GROUP, C_VA), row(LANES, 0),
                  pl.BlockSpec((1, hd), lambda i: (0, 0))],
        out_specs=(row(GROUP, 0), row(GROUP, 0), row(GROUP, 0), row(GROUP, 0), row(LANES, 0), row(LANES, 0)),
        compiler_params=_cparams(("arbitrary",)),
        name="dsa_prep",
    )(p, p, ps, gk)


def _dsa_prompt_kernel(q_ref, qi_ref, ps_ref, gq_ref, kie_ref, kio_ref, kb_ref, vb_ref,
                       ha_ref, key_ref, wb_ref, *, tq, topk):
    i = pl.program_id(0)
    hd = gq_ref.shape[1]
    ck = tq
    nk = i + 1
    ncol = ck // LANES
    kf = float(topk)

    ps = ps_ref[...]
    w_scale = (IDX_HEADS ** -0.5) * (IDX_DIM ** -0.5)
    for h in range(IDX_HEADS):
        wb_ref[h] = jnp.broadcast_to(ps[:, S_WIDX + h:S_WIDX + h + 1] * w_scale, (tq, LANES))
    qi = qi_ref[...].astype(BF16)

    row_pos = i * tq + lax.broadcasted_iota(I32, (tq, ck), 0)
    col_in = lax.broadcasted_iota(I32, (tq, ck), 1)

    def score_body(c, carry):
        off = pl.multiple_of(c * ck, ck)
        ke = kie_ref[pl.ds(off, ck), :]
        ko = kio_ref[pl.ds(off, ck), :]
        acc = jnp.zeros((tq, ck), F32)
        for pr in range(IDX_HEADS // 2):
            q2 = qi[:, pr * LANES:(pr + 1) * LANES]
            se = _dot_nt(q2, ke)
            so = _dot_nt(q2, ko)
            acc = acc + jnp.maximum(se, 0.0) * jnp.tile(wb_ref[2 * pr], (1, ncol))
            acc = acc + jnp.maximum(so, 0.0) * jnp.tile(wb_ref[2 * pr + 1], (1, ncol))
        key = jnp.where(col_in + c * ck <= row_pos, _mono_key(acc), INT_MIN)
        key_ref[:, pl.ds(off, ck)] = key
        return carry

    lax.fori_loop(0, nk, score_body, 0)

    def count_where(pred):
        def body(c, acc):
            off = pl.multiple_of(c * ck, ck)
            kc = key_ref[:, pl.ds(off, ck)]
            for j in range(ncol):
                acc = acc + jnp.where(pred(kc[:, j * LANES:(j + 1) * LANES], c * ck + j * LANES), 1.0, 0.0)
            return acc
        acc = lax.fori_loop(0, nk, body, jnp.zeros((tq, LANES), F32))
        return jnp.sum(acc, axis=-1, keepdims=True)

    def count_ge(cand):
        cb = jnp.broadcast_to(cand, (tq, LANES))
        return count_where(lambda kc, base: kc >= cb)

    cnt0 = count_ge(jnp.zeros((tq, 1), I32))
    ok0 = cnt0 >= kf
    thr0 = jnp.where(ok0, 0, INT_MIN).astype(I32)
    cnt_t0 = jnp.where(ok0, cnt0, kf)

    def bit_body(b, carry):
        thr, cnt_t = carry
        cand = thr + jnp.left_shift(jnp.int32(1), 30 - b)
        cnt = count_ge(cand)
        ok = cnt >= kf
        return jnp.where(ok, cand, thr), jnp.where(ok, cnt, cnt_t)

    thr, cnt_t = lax.fori_loop(0, 31, bit_body, (thr0, cnt_t0))
    has_tie = jnp.max(cnt_t) > kf
    bias_on, bias_off = 0.0, -jnp.inf

    @pl.when(jnp.logical_not(has_tie))
    def _():
        tb = jnp.broadcast_to(jnp.maximum(thr, INT_MIN + 1), (tq, LANES))

        def body(c, carry):
            off = pl.multiple_of(c * ck, ck)
            kc = key_ref[:, pl.ds(off, ck)]
            bias = jnp.where(kc >= jnp.tile(tb, (1, ncol)), bias_on, bias_off)
            key_ref[:, pl.ds(off, ck)] = lax.bitcast_convert_type(bias, I32)
            return carry
        lax.fori_loop(0, nk, body, 0)

    @pl.when(has_tie)
    def _():
        tb = jnp.broadcast_to(thr, (tq, LANES))
        lane = lax.broadcasted_iota(I32, (tq, LANES), 1)
        n_gt = count_where(lambda kc, base: kc > tb)
        need = kf - n_gt

        def pos_body(b, x):
            cand = x + jnp.left_shift(jnp.int32(1), b)
            cb = jnp.broadcast_to(cand, (tq, LANES))
            cnt = count_where(lambda kc, base: (kc == tb) & (lane + base < cb))
            return jnp.where(cnt < need, cand, x)

        nbits = int(math.ceil(math.log2(key_ref.shape[1]))) + 1
        x = lax.fori_loop(0, nbits, lambda b, x: pos_body(nbits - 1 - b, x), jnp.zeros((tq, 1), I32))
        x = jnp.where(thr > INT_MIN, x, -1)
        xb = jnp.broadcast_to(x, (tq, LANES))

        def body(c, carry):
            off = pl.multiple_of(c * ck, ck)
            for j in range(ncol):
                kc = key_ref[:, pl.ds(off + j * LANES, LANES)]
                sel = (kc > tb) | ((kc == tb) & (lane + (c * ck + j * LANES) <= xb))
                bias = jnp.where(sel, bias_on, bias_off)
                key_ref[:, pl.ds(off + j * LANES, LANES)] = lax.bitcast_convert_type(bias, I32)
            return carry
        lax.fori_loop(0, nk, body, 0)

    qscale = (hd ** -0.5) * LOG2E
    for h in range(A_HEADS):
        hs = slice(h * hd, (h + 1) * hd)
        qh = (_rms(q_ref[:, hs], gq_ref[...]) * qscale).astype(BF16)

        def att_body(c, carry, hs=hs, qh=qh):
            m, l, acc = carry
            off = pl.multiple_of(c * ck, ck)
            s = _dot_nt(qh, kb_ref[pl.ds(off, ck), hs])
            s = s + lax.bitcast_convert_type(key_ref[:, pl.ds(off, ck)], F32)
            m_new = jnp.maximum(m, jnp.max(s, axis=-1, keepdims=True))
            alpha = jnp.exp2(m - m_new)
            p = jnp.exp2(s - m_new)
            l = alpha * l + jnp.sum(p, axis=-1, keepdims=True)
            acc = alpha * acc + _dot(p.astype(BF16), vb_ref[pl.ds(off, ck), hs])
            return m_new, l, acc

        init = (jnp.full((tq, 1), NEG_INIT, F32), jnp.zeros((tq, 1), F32), jnp.zeros((tq, hd), F32))
        _, l, acc = lax.fori_loop(0, nk, att_body, init)
        ha_ref[:, hs] = (acc / l).astype(BF16)


def _dsa_prompt(p, ps, gq, kie, kio, kb, vb, *, tq):
    t = p.shape[0]
    hd = gq.shape[1]
    topk = min(TOPK_MAX, t // 4)
    return pl.pallas_call(
        functools.partial(_dsa_prompt_kernel, tq=tq, topk=topk),
        out_shape=jax.ShapeDtypeStruct((t, GROUP), BF16),
        grid=(t // tq,),
        in_specs=[
            pl.BlockSpec((tq, GROUP), lambda i: (i, C_QA)),
            pl.BlockSpec((tq, GROUP), lambda i: (i, C_QI)),
            pl.BlockSpec((tq, LANES), lambda i: (i, 0)),
            pl.BlockSpec((1, hd), lambda i: (0, 0)),
            _vmem_spec(), _vmem_spec(), _vmem_spec(), _vmem_spec(),
        ],
        out_specs=pl.BlockSpec((tq, GROUP), lambda i: (i, 0)),
        scratch_shapes=[pltpu.VMEM((tq, t), I32), pltpu.VMEM((IDX_HEADS, tq, LANES), F32)],
        compiler_params=_cparams(("arbitrary",)),
        name="dsa_prompt",
    )(p, p, ps, gq, kie, kio, kb, vb)


PAGES_PER_STEP = 16


def _dsa_sample_prep_kernel(q_ref, k_ref, v_ref, qi_ref, ps_ref, gq_ref, gk_ref,
                            qs_ref, kn_ref, vn_ref, sn_ref):
    hd = gq_ref.shape[1]
    qscale = (hd ** -0.5) * LOG2E
    for h in range(A_HEADS):
        hs = slice(h * hd, (h + 1) * hd)
        qs_ref[:, hs] = _rms(q_ref[:, hs], gq_ref[...]) * qscale
        kn_ref[:, hs] = _rms(k_ref[:, hs], gk_ref[...])
    vn_ref[...] = v_ref[...]
    ps = ps_ref[...]
    lane = lax.broadcasted_iota(I32, ps.shape, 1)
    ke = jnp.where(lane < IDX_DIM, ps, 0.0)
    ko = pltpu.roll(ke, IDX_DIM, axis=1)
    w_scale = (IDX_HEADS ** -0.5) * (IDX_DIM ** -0.5)
    acc = jnp.zeros((ps.shape[0], 1), F32)
    for pr in range(IDX_HEADS // 2):
        q2 = qi_ref[:, pr * LANES:(pr + 1) * LANES]
        se = jnp.sum(q2 * ke, axis=-1, keepdims=True)
        so = jnp.sum(q2 * ko, axis=-1, keepdims=True)
        acc = acc + jnp.maximum(se, 0.0) * (ps[:, S_WIDX + 2 * pr:S_WIDX + 2 * pr + 1] * w_scale)
        acc = acc + jnp.maximum(so, 0.0) * (ps[:, S_WIDX + 2 * pr + 1:S_WIDX + 2 * pr + 2] * w_scale)
    sn_ref[...] = jnp.broadcast_to(acc, sn_ref.shape)


def _dsa_sample_prep(p, ps, gq, gk):
    db = p.shape[0]
    hd = gq.shape[1]
    blk = lambda c: pl.BlockSpec((db, GROUP), lambda i: (0, c))
    full = lambda shape: pl.BlockSpec(shape, lambda i: (0,) * len(shape))
    return pl.pallas_call(
        _dsa_sample_prep_kernel,
        out_shape=(jax.ShapeDtypeStruct((db, GROUP), F32), jax.ShapeDtypeStruct((db, GROUP), F32),
                   jax.ShapeDtypeStruct((db, GROUP), F32), jax.ShapeDtypeStruct((db, LANES), F32)),
        grid=(1,),
        in_specs=[blk(C_QA), blk(C_KA), blk(C_VA), blk(C_QI), full((db, LANES)), full((1, hd)), full((1, hd))],
        out_specs=(full((db, GROUP)), full((db, GROUP)), full((db, GROUP)), full((db, LANES))),
        compiler_params=_cparams(("arbitrary",)),
        name="dsa_sample_prep",
    )(p, p, p, p, ps, gq, gk)


def _dsa_sample_scores_kernel(pt_ref, qi_ref, w_ref, *refs):
    page_refs, out_ref = refs[:PAGES_PER_STEP], refs[PAGES_PER_STEP]
    qi = qi_ref[0].astype(BF16)
    w = w_ref[0]
    for r in range(PAGES_PER_STEP):
        ki = page_refs[r][0, 0].astype(BF16)
        s = jnp.maximum(_dot_nt(qi, ki), 0.0) * w
        out_ref[0, :, r * PAGE_SIZE:(r + 1) * PAGE_SIZE] = jnp.sum(s, axis=0, keepdims=True)


def _dsa_sample_scores(page_table, qi3, w3, cache_kidx):
    db, n_pages = page_table.shape
    steps = n_pages // PAGES_PER_STEP

    def page_spec(r):
        return pl.BlockSpec((1, 1, PAGE_SIZE, IDX_DIM),
                            lambda b, j, pt: (0, pt[b, j * PAGES_PER_STEP + r], 0, 0))

    grid_spec = pltpu.PrefetchScalarGridSpec(
        num_scalar_prefetch=1,
        grid=(db, steps),
        in_specs=[pl.BlockSpec((1, IDX_HEADS, IDX_DIM), lambda b, j, pt: (b, 0, 0)),
                  pl.BlockSpec((1, IDX_HEADS, 1), lambda b, j, pt: (b, 0, 0))]
                 + [page_spec(r) for r in range(PAGES_PER_STEP)],
        out_specs=pl.BlockSpec((1, 1, PAGES_PER_STEP * PAGE_SIZE), lambda b, j, pt: (b * steps + j, 0, 0)),
    )
    return pl.pallas_call(
        _dsa_sample_scores_kernel,
        out_shape=jax.ShapeDtypeStruct((db * steps, 1, PAGES_PER_STEP * PAGE_SIZE), F32),
        grid_spec=grid_spec,
        compiler_params=_cparams(("arbitrary", "arbitrary")),
        name="dsa_sample_scores",
    )(page_table, qi3, w3, *([cache_kidx] * PAGES_PER_STEP))


def _dsa_sample_select_kernel(sc_ref, sn_ref, idx_ref, nv_ref, ns_ref, *, topk):
    db, npg, pgs = sc_ref.shape
    kf = float(topk)
    key = _mono_key(sc_ref[...])
    key_n = _mono_key(sn_ref[...][:, 0:1]).reshape(db, 1, 1)

    def total(x):
        return jnp.sum(jnp.sum(x, axis=2, keepdims=True), axis=1, keepdims=True)

    def count_ge(cand):
        return (total(jnp.where(key >= cand, 1.0, 0.0)) + jnp.where(key_n >= cand, 1.0, 0.0))

    cnt0 = count_ge(jnp.zeros((db, 1, 1), I32))
    thr0 = jnp.where(cnt0 >= kf, 0, INT_MIN).astype(I32)

    def bit_body(b, thr):
        cand = thr + jnp.left_shift(jnp.int32(1), 30 - b)
        return jnp.where(count_ge(cand) >= kf, cand, thr)

    thr = lax.fori_loop(0, 31, bit_body, thr0)
    gt = jnp.where(key > thr, 1.0, 0.0)
    eq = jnp.where(key == thr, 1.0, 0.0)
    gt_n = jnp.where(key_n > thr, 1.0, 0.0)
    eq_n = jnp.where(key_n == thr, 1.0, 0.0)
    need = kf - (total(gt) + gt_n)

    r_i = lax.broadcasted_iota(I32, (pgs, pgs), 0)
    c_i = lax.broadcasted_iota(I32, (pgs, pgs), 1)
    upper = jnp.where(r_i <= c_i, 1.0, 0.0).astype(BF16)
    lower = jnp.where(c_i <= r_i, 1.0, 0.0).astype(BF16)
    lower_pg = jnp.where(lax.broadcasted_iota(I32, (npg, npg), 1)
                         <= lax.broadcasted_iota(I32, (npg, npg), 0), 1.0, 0.0).astype(BF16)
    j_row = lax.broadcasted_iota(I32, (1, topk), 1).astype(F32)
    p_col = lax.broadcasted_iota(I32, (npg, 1), 0).astype(F32)

    def page_prefix(x):
        pin = _dot(x.astype(BF16), upper)
        rt = pin[:, pgs - 1:pgs]
        incl = _dot(lower_pg, jnp.broadcast_to(rt, (npg, pgs)).astype(BF16))[:, 0:1]
        return pin, rt, incl

    for b in range(db):
        eq_b = eq[b]
        pin_e, rt_e, incl_e = page_prefix(eq_b)
        rank_e = (incl_e - rt_e) + pin_e
        need_b = need[b]
        sel = gt[b] + jnp.where(rank_e <= need_b, eq_b, 0.0)
        n_eq = incl_e[npg - 1:npg, :]
        new_sel = gt_n[b] + jnp.where(n_eq + 1.0 <= need_b, eq_n[b], 0.0)

        pin, rt, incl = page_prefix(sel)
        ex = incl - rt
        n_sel = incl[npg - 1:npg, :]
        page_j = jnp.sum(jnp.where(incl <= j_row, 1.0, 0.0), axis=0, keepdims=True)
        oh_t = p_col == page_j
        ex_j = jnp.sum(jnp.where(oh_t, ex, 0.0), axis=0, keepdims=True)
        r_j = j_row - ex_j + 1.0
        pin_t = _dot(lower, sel.T.astype(BF16))
        rows_t = _dot(pin_t.astype(BF16), jnp.where(oh_t, 1.0, 0.0).astype(BF16))
        off_j = jnp.sum(jnp.where(rows_t < r_j, 1.0, 0.0), axis=0, keepdims=True)
        idx = jnp.where(j_row < n_sel, page_j * float(pgs) + off_j, 0.0)
        idx_ref[b:b + 1, :] = idx.astype(I32)
        nv_ref[b:b + 1, :] = jnp.broadcast_to(n_sel, (1, LANES)).astype(I32)
        ns_ref[b:b + 1, :] = jnp.broadcast_to(new_sel, (1, LANES)).astype(I32)


def _dsa_sample_select(sc3, sn, *, topk):
    db = sc3.shape[0]
    return pl.pallas_call(
        functools.partial(_dsa_sample_select_kernel, topk=topk),
        out_shape=(jax.ShapeDtypeStruct((db, topk), I32), jax.ShapeDtypeStruct((db, LANES), I32),
                   jax.ShapeDtypeStruct((db, LANES), I32)),
        in_specs=[_vmem_spec(), _vmem_spec()],
        out_specs=(_vmem_spec(), _vmem_spec(), _vmem_spec()),
        compiler_params=_cparams(None),
        name="dsa_sample_select",
    )(sc3, sn)


def _dsa_sample_attend_kernel(idx_ref, pt_ref, nv_ref, ns_ref, q_ref, kn_ref, vn_ref, ck_ref, cv_ref,
                              o_ref, kbuf, vbuf, sem, *, topk, nseq):
    b = pl.program_id(0)
    slot = b % 2
    nh = A_HEADS
    hd = q_ref.shape[2] // nh

    def row_copies(seq, j, sl):
        r = idx_ref[seq, j]
        pg = pt_ref[seq, lax.shift_right_logical(r, 7)]
        off = r & (PAGE_SIZE - 1)
        dst = pl.ds(pl.multiple_of(j * nh, nh), nh)
        return (pltpu.make_async_copy(ck_ref.at[0, pg, off], kbuf.at[sl, dst], sem.at[0, sl]),
                pltpu.make_async_copy(cv_ref.at[0, pg, off], vbuf.at[sl, dst], sem.at[1, sl]))

    def issue(seq, sl):
        def body(j, carry):
            ck, cv = row_copies(seq, j, sl)
            ck.start()
            cv.start()
            return carry
        lax.fori_loop(0, topk, body, 0)

    @pl.when(b == 0)
    def _():
        issue(0, 0)

    @pl.when(b + 1 < nseq)
    def _():
        issue(b + 1, 1 - slot)

    def wait_body(j, carry):
        ck, cv = row_copies(b, j, slot)
        ck.wait()
        cv.wait()
        return carry
    lax.fori_loop(0, topk, wait_body, 0)

    n_valid = nv_ref[b]
    new_on = ns_ref[b] > 0
    j_row = lax.broadcasted_iota(I32, (1, topk), 1)
    for h in range(nh):
        hs = slice(h * hd, (h + 1) * hd)
        qh = q_ref[0, :, hs]
        kh = kbuf[slot, pl.ds(h, topk, stride=nh), :]
        vh = vbuf[slot, pl.ds(h, topk, stride=nh), :]
        s = _dot_nt(qh.astype(BF16), kh.astype(BF16))
        s = jnp.where(j_row < n_valid, s, -jnp.inf)
        s_new = jnp.sum(qh * kn_ref[0, :, hs], axis=-1, keepdims=True)
        s_new = jnp.where(new_on, s_new, -jnp.inf)
        m = jnp.maximum(jnp.max(s, axis=-1, keepdims=True), s_new)
        p = jnp.exp2(s - m)
        p_new = jnp.exp2(s_new - m)
        l = jnp.sum(p, axis=-1, keepdims=True) + p_new
        o = _dot(p.astype(BF16), vh.astype(BF16)) + p_new * vn_ref[0, :, hs]
        o_ref[0, :, hs] = o / l


def _dsa_sample_attend(idx, page_table, nv, ns, qs3, kn3, vn3, cache_k, cache_v, *, topk):
    db = idx.shape[0]
    w = qs3.shape[2]
    hd = w // A_HEADS
    row = pl.BlockSpec((1, 1, w), lambda b, *_: (b, 0, 0))
    grid_spec = pltpu.PrefetchScalarGridSpec(
        num_scalar_prefetch=4,
        grid=(db,),
        in_specs=[row, row, row, pl.BlockSpec(memory_space=pl.ANY), pl.BlockSpec(memory_space=pl.ANY)],
        out_specs=row,
        scratch_shapes=[pltpu.VMEM((2, topk * A_HEADS, hd), F32), pltpu.VMEM((2, topk * A_HEADS, hd), F32),
                        pltpu.SemaphoreType.DMA((2, 2))],
    )
    return pl.pallas_call(
        functools.partial(_dsa_sample_attend_kernel, topk=topk, nseq=db),
        out_shape=jax.ShapeDtypeStruct((db, 1, w), F32),
        grid_spec=grid_spec,
        compiler_params=_cparams(("arbitrary",)),
        name="dsa_sample_attend",
    )(idx, page_table, nv, ns, qs3, kn3, vn3, cache_k, cache_v)


def _merge_kernel(hm_ref, ha_ref, gm_ref, ga_ref, x_ref, wmo_ref, wao_ref, wout_ref, y_ref):
    t1 = _dot(hm_ref[...].astype(BF16), wmo_ref[...])
    t2 = _dot(ha_ref[...].astype(BF16), wao_ref[...])
    z = _sigmoid(gm_ref[...]) * t1 + _sigmoid(ga_ref[...]) * t2
    y_ref[...] = x_ref[...] + _dot(z.astype(BF16), wout_ref[...])


def _merge(hm, ha, p, x, wmo, wao, wout, *, tm):
    m, d = x.shape
    w = hm.shape[1]
    return pl.pallas_call(
        _merge_kernel,
        out_shape=jax.ShapeDtypeStruct((m, d), F32),
        grid=(pl.cdiv(m, tm),),
        in_specs=[
            pl.BlockSpec((tm, w), lambda i: (i, 0)),
            pl.BlockSpec((tm, w), lambda i: (i, 0)),
            pl.BlockSpec((tm, d), lambda i: (i, 0)),
            pl.BlockSpec((tm, d), lambda i: (i, 1)),
            pl.BlockSpec((tm, d), lambda i: (i, 0)),
            _vmem_spec(), _vmem_spec(), _vmem_spec(),
        ],
        out_specs=pl.BlockSpec((tm, d), lambda i: (i, 0)),
        compiler_params=_cparams(("arbitrary",)),
        name="merge",
    )(hm, ha, p, p, x, wmo, wao, wout)


FFN_TM = 512
FFN_TF = 512
PROJ_TM = 512
MERGE_TM = 256
MLSTM_ROWS = 256
MLSTM_SAMPLE_GROUP = 8
DSA_TQ = 256
DSA_PREP_TM = 512


def kernel(x_prompt, x_sample, cache_k, cache_v, cache_kidx, state_conv, state_C, state_n, state_m, page_table, norm_ffn1, w_ffn1_up, w_ffn1_down, norm_mix, w_in, b_if, w_conv, b_conv, w_qk_m, norm_mlstm_out, norm_q, norm_k, w_mo, w_ao, w_out, norm_ffn2, w_ffn2_up, w_ffn2_down):
    depth = w_in.shape[0]
    assert depth == 1, "single-layer step"
    bsz, seq, d = x_prompt.shape
    assert bsz == 1
    db, dseq, _ = x_sample.shape
    assert dseq == 1
    n_pages = page_table.shape[1]
    assert cache_k.shape[2] == PAGE_SIZE and n_pages % PAGES_PER_STEP == 0
    past = n_pages * PAGE_SIZE
    topk_s = min(TOPK_MAX, (past + dseq) // 4)
    l = 0

    xp = x_prompt.reshape(seq, d)
    xs = x_sample.reshape(db, d)

    wu1, wd1 = _prep_ffn_weights(w_ffn1_up[l], w_ffn1_down[l], FFN_TF)
    wu2, wd2 = _prep_ffn_weights(w_ffn2_up[l], w_ffn2_down[l], FFN_TF)
    w_big, w_small = _prep_w_in(w_in[l], d)
    wmo, wao, wout = w_mo[l].astype(BF16), w_ao[l].astype(BF16), w_out[l].astype(BF16)
    wqk = w_qk_m[l].astype(BF16)
    g1, gmix, g2 = norm_ffn1[l][None], norm_mix[l][None], norm_ffn2[l][None]
    gq, gk = norm_q[l][None], norm_k[l][None]
    gmo = norm_mlstm_out[l]
    bif_row = jnp.zeros((1, LANES), F32).at[0, S_IGATE:S_IGATE + 2 * M_HEADS].set(b_if[l])
    wcv, bcv = w_conv[l], b_conv[l][None]

    xp1 = _ffn(xp, g1, wu1, wd1, tm=FFN_TM, tf=FFN_TF)
    xs1 = _ffn(xs, g1, wu1, wd1, tm=db, tf=FFN_TF)

    pp, pps = _inproj(xp1, gmix, w_big, w_small, tm=PROJ_TM)
    sp, sps = _inproj(xs1, gmix, w_big, w_small, tm=db)

    hm_p, conv_p, c_p, n_p, m_p = _mlstm_prompt(pp, pps, bif_row, wcv, bcv, wqk, gmo, rows=MLSTM_ROWS)
    k32, v32, kb, vb, kie, kio = _dsa_prep(pp, pps, gk, tm=DSA_PREP_TM)
    ha_p = _dsa_prompt(pp, pps, gq, kie, kio, kb, vb, tq=DSA_TQ)
    yp = _merge(hm_p, ha_p, pp, xp1, wmo, wao, wout, tm=MERGE_TM)
    out_p = _ffn(yp, g2, wu2, wd2, tm=FFN_TM, tf=FFN_TF)

    conv_t = jnp.transpose(state_conv[l], (1, 0, 2))
    hm_s, conv_s, c_s, n_s, m_s = _mlstm_sample(sp, sps, conv_t, state_C[l], state_n[l], state_m[l],
                                                bif_row, wcv, bcv, wqk, gmo, gb=MLSTM_SAMPLE_GROUP)
    qs, kn_s, vn_s, sn = _dsa_sample_prep(sp, sps, gq, gk)
    qi3 = sp[:, C_QI * GROUP:(C_QI + 1) * GROUP].reshape(db, IDX_HEADS, IDX_DIM)
    w3 = (sps[:, S_WIDX:S_WIDX + IDX_HEADS] * ((IDX_HEADS ** -0.5) * (IDX_DIM ** -0.5))).reshape(db, IDX_HEADS, 1)
    sc = _dsa_sample_scores(page_table, qi3, w3, cache_kidx)
    sc3 = sc.reshape(db, n_pages, PAGE_SIZE)
    idx, nv, ns = _dsa_sample_select(sc3, sn, topk=topk_s)
    ha_s = _dsa_sample_attend(idx, page_table, nv[:, 0], ns[:, 0], qs.reshape(db, 1, GROUP),
                              kn_s.reshape(db, 1, GROUP), vn_s.reshape(db, 1, GROUP),
                              cache_k, cache_v, topk=topk_s)
    ys = _merge(hm_s, ha_s.reshape(db, GROUP), sp, xs1, wmo, wao, wout, tm=db)
    out_s = _ffn(ys, g2, wu2, wd2, tm=db, tf=FFN_TF)

    hd = GROUP // A_HEADS
    hv = GROUP // M_HEADS
    hk = hv // 2
    return (
        out_p.reshape(bsz, seq, d),
        out_s.reshape(db, dseq, d),
        k32.reshape(depth, bsz, seq, A_HEADS, hd),
        v32.reshape(depth, bsz, seq, A_HEADS, hd),
        pps[:, S_KIDX:S_KIDX + IDX_DIM].reshape(depth, bsz, seq, IDX_DIM),
        conv_p.reshape(depth, bsz, CONV_W - 1, GROUP),
        c_p.reshape(depth, bsz, M_HEADS, hk, hv),
        n_p[:M_HEADS].reshape(depth, bsz, M_HEADS, hk),
        m_p[:M_HEADS, 0].reshape(depth, bsz, M_HEADS),
        kn_s.reshape(depth, db, dseq, A_HEADS, hd),
        vn_s.reshape(depth, db, dseq, A_HEADS, hd),
        sps[:, S_KIDX:S_KIDX + IDX_DIM].reshape(depth, db, dseq, IDX_DIM),
        jnp.transpose(conv_s, (1, 0, 2)).reshape(depth, db, CONV_W - 1, GROUP),
        c_s.reshape(depth, db, M_HEADS, hk, hv),
        n_s.reshape(depth, db, M_HEADS, hk),
        m_s.reshape(depth, db, M_HEADS),
    )
```

```python
import functools
import math

import jax
import jax.numpy as jnp
from jax import lax
from jax.experimental import pallas as pl
from jax.experimental.pallas import tpu as pltpu

F32 = jnp.float32
BF16 = jnp.bfloat16
I32 = jnp.int32

EPS = 1e-6
LANES = 128
PAGE_SIZE = 128
M_HEADS = 4
M_CHUNK = 64
CONV_W = 4
A_HEADS = 8
IDX_HEADS = 16
IDX_DIM = 64
TOPK_MAX = 256
N_BRANCHES = 2
GROUP = 1024
INT_MIN = -2 ** 31
NEG_INIT = -1e30
LOG2E = 1.4426950408889634
VMEM_LIMIT = 56 * 1024 * 1024

S_KIDX = 0
S_WIDX = IDX_DIM
S_IGATE = IDX_DIM + IDX_HEADS
S_FGATE = S_IGATE + M_HEADS

C_GATE, C_UM, C_VM, C_OM, C_QA, C_KA, C_VA, C_QI = 0, 4, 5, 6, 7, 8, 9, 10
N_BIG_BLOCKS = 11


def _cparams(sem, vmem=VMEM_LIMIT):
    return pltpu.CompilerParams(dimension_semantics=sem, vmem_limit_bytes=vmem)


def _vmem_spec():
    return pl.BlockSpec(memory_space=pltpu.VMEM)


def _rms(x, g):
    return x * lax.rsqrt(jnp.mean(x * x, axis=-1, keepdims=True) + EPS) * g


def _sigmoid(x):
    return 1.0 / (1.0 + jnp.exp(-x))


def _log_sigmoid(x):
    return jnp.minimum(x, 0.0) - jnp.log(1.0 + jnp.exp(-jnp.abs(x)))


def _dot(a, b):
    return jnp.dot(a, b, preferred_element_type=F32)


def _dot_nt(a, b):
    return lax.dot_general(a, b, (((1,), (1,)), ((), ())), preferred_element_type=F32)


def _dot_exact(a, b):
    return jnp.dot(a, b, preferred_element_type=F32, precision=lax.Precision.HIGHEST)


def _mono_key(x):
    b = lax.bitcast_convert_type(x, I32)
    return b ^ ((b >> 31) & jnp.int32(0x7FFFFFFF))


def _ffn_kernel(x_ref, g_ref, wa_ref, wb_ref, wd_ref, wat_ref, wbt_ref, wdt_ref, o_ref, h_ref, *, nfull):
    j = pl.program_id(1)

    @pl.when(j == 0)
    def _():
        h_ref[...] = _rms(x_ref[...], g_ref[...]).astype(BF16)

    def half_step(wa, wb, wd):
        h = h_ref[...]
        a = _dot(h, wa[...])
        act = (a * _sigmoid(a) * _dot(h, wb[...])).astype(BF16)
        return _dot(act, wd[...])

    @pl.when(j == 0)
    def _():
        o_ref[...] = half_step(wa_ref, wb_ref, wd_ref)

    @pl.when((j > 0) & (j < nfull))
    def _():
        o_ref[...] += half_step(wa_ref, wb_ref, wd_ref)

    @pl.when(j == nfull)
    def _():
        o_ref[...] = x_ref[...] + 0.5 * (o_ref[...] + half_step(wat_ref, wbt_ref, wdt_ref))


def _ffn(x, g, w_up, w_down, *, tm, tf):
    m, d = x.shape
    d_ff = w_down.shape[0]
    nfull = d_ff // tf
    tail = d_ff - nfull * tf
    assert tail > 0 and tail % LANES == 0 and d_ff % LANES == 0
    elem = lambda r, c: (pl.Element(r), pl.Element(c))
    main = lambda j: jnp.minimum(j, nfull - 1) * tf
    main_b = lambda j: (d_ff // LANES + jnp.minimum(j, nfull - 1) * (tf // LANES)) * LANES
    tail_b = (d_ff + nfull * tf) // LANES
    return pl.pallas_call(
        functools.partial(_ffn_kernel, nfull=nfull),
        out_shape=jax.ShapeDtypeStruct((m, d), F32),
        grid=(pl.cdiv(m, tm), nfull + 1),
        in_specs=[
            pl.BlockSpec((tm, d), lambda i, j: (i, 0)),
            pl.BlockSpec((1, d), lambda i, j: (0, 0)),
            pl.BlockSpec(elem(d, tf), lambda i, j: (0, main(j))),
            pl.BlockSpec(elem(d, tf), lambda i, j: (0, main_b(j))),
            pl.BlockSpec(elem(tf, d), lambda i, j: (main(j), 0)),
            pl.BlockSpec(elem(d, tail), lambda i, j: (0, nfull * tf)),
            pl.BlockSpec(elem(d, tail), lambda i, j: (0, tail_b * LANES)),
            pl.BlockSpec(elem(tail, d), lambda i, j: (nfull * tf, 0)),
        ],
        out_specs=pl.BlockSpec((tm, d), lambda i, j: (i, 0)),
        scratch_shapes=[pltpu.VMEM((tm, d), BF16)],
        compiler_params=_cparams(("arbitrary", "arbitrary")),
        name="ffn",
    )(x, g, w_up, w_up, w_down, w_up, w_up, w_down)


INPROJ_WINDOW = GROUP + LANES


def _inproj_kernel(tab_ref, x_ref, g_ref, w_ref, ws_ref, p_ref, ps_ref, h_ref):
    j = pl.program_id(1)

    @pl.when(j == 0)
    def _():
        h = _rms(x_ref[...], g_ref[...]).astype(BF16)
        h_ref[...] = h
        ps_ref[...] = _dot(h, ws_ref[...])

    res = _dot(h_ref[...], w_ref[...])
    p_ref[...] = pltpu.roll(res, tab_ref[1, j], axis=1)[:, :GROUP]


def _inproj(x, g, w_pad, w_small, tab, *, tm):
    m, d = x.shape
    n_groups = tab.shape[1]
    grid_spec = pltpu.PrefetchScalarGridSpec(
        num_scalar_prefetch=1,
        grid=(pl.cdiv(m, tm), n_groups),
        in_specs=[
            pl.BlockSpec((tm, d), lambda i, j, tab: (i, 0)),
            pl.BlockSpec((1, d), lambda i, j, tab: (0, 0)),
            pl.BlockSpec((pl.Element(d), pl.Element(INPROJ_WINDOW)), lambda i, j, tab: (0, tab[0, j] * LANES)),
            pl.BlockSpec((d, LANES), lambda i, j, tab: (0, 0)),
        ],
        out_specs=(pl.BlockSpec((tm, GROUP), lambda i, j, tab: (i, j)),
                   pl.BlockSpec((tm, LANES), lambda i, j, tab: (i, 0))),
        scratch_shapes=[pltpu.VMEM((tm, d), BF16)],
    )
    return pl.pallas_call(
        _inproj_kernel,
        out_shape=(jax.ShapeDtypeStruct((m, n_groups * GROUP), F32), jax.ShapeDtypeStruct((m, LANES), F32)),
        grid_spec=grid_spec,
        compiler_params=_cparams(("arbitrary", "arbitrary")),
        name="inproj",
    )(tab, x, g, w_pad, w_small)


def _prep_w_in(w_in, d_model):
    mw = GROUP
    sizes = (mw, mw, mw, M_HEADS, M_HEADS, GROUP, GROUP, GROUP, IDX_HEADS * IDX_DIM, IDX_DIM,
             IDX_HEADS, N_BRANCHES * d_model)
    offs = [0]
    for s in sizes:
        offs.append(offs[-1] + s)
    u_m, v_m, o_m, i_m, f_m, q_a, k_a, v_a, q_i, k_i, w_i, gate = range(12)
    starts = [offs[gate] + r * GROUP for r in range(N_BRANCHES * d_model // GROUP)]
    starts += [offs[k] for k in (u_m, v_m, o_m, q_a, k_a, v_a, q_i)]
    assert len(starts) == N_BIG_BLOCKS
    aligned = [s // LANES * LANES for s in starts]
    rolls = [(INPROJ_WINDOW - (s - a)) % INPROJ_WINDOW for s, a in zip(starts, aligned)]
    tab = jnp.array([[a // LANES for a in aligned], rolls], I32)
    width = max(aligned) + INPROJ_WINDOW
    w16 = w_in.astype(BF16)
    w_pad = jnp.pad(w16, ((0, 0), (0, width - w_in.shape[1])))
    col = lambda k: w16[:, offs[k]:offs[k + 1]]
    fill = jnp.zeros((w_in.shape[0], LANES - S_FGATE - M_HEADS), BF16)
    w_small = jnp.concatenate([col(k_i), col(w_i), col(i_m), col(f_m), fill], axis=1)
    return w_pad, w_small, tab


def _mlstm_prompt_kernel(u_ref, v_ref, o_ref, ps_ref, bif_ref, wconv_ref, bconv_ref, wqk_ref, gmo_ref,
                         hm_ref, conv_ref, cst_ref, nst_ref, mst_ref,
                         ext_ref, c_ref, n_ref, m_ref, *, rows, nsteps):
    i = pl.program_id(0)
    mw = u_ref.shape[1]
    hv = mw // M_HEADS
    hk = hv // 2
    L = M_CHUNK

    @pl.when(i == 0)
    def _():
        ext_ref[0:8, :] = jnp.zeros((8, mw), F32)
        c_ref[...] = jnp.zeros(c_ref.shape, F32)
        n_ref[...] = jnp.zeros(n_ref.shape, F32)
        m_ref[...] = jnp.zeros(m_ref.shape, F32)

    u = u_ref[...]
    ext_ref[8:8 + rows, :] = u
    wc = wconv_ref[...]
    cv = (ext_ref[5:5 + rows, :] * wc[0:1] + ext_ref[6:6 + rows, :] * wc[1:2]
          + ext_ref[7:7 + rows, :] * wc[2:3] + u * wc[3:4]) + bconv_ref[...]
    ext_ref[0:8, :] = u[rows - 8:rows, :]
    cb = (cv * _sigmoid(cv)).astype(BF16)

    g = ps_ref[...] + bif_ref[...]
    lf = _log_sigmoid(g)
    r_i = lax.broadcasted_iota(I32, (rows, rows), 0)
    c_i = lax.broadcasted_iota(I32, (rows, rows), 1)
    tri = jnp.where((r_i // L == c_i // L) & (c_i <= r_i), 1.0, 0.0).astype(F32)
    bcum = _dot_exact(tri, lf)
    g_t = g.T
    b_t = bcum.T
    causal = lax.broadcasted_iota(I32, (L, L), 1) <= lax.broadcasted_iota(I32, (L, L), 0)

    for h in range(M_HEADS):
        hs = slice(h * hv, (h + 1) * hv)
        qk = _dot(cb[:, hs], wqk_ref[h])
        q = qk[:, :hk] * (hk ** -0.5)
        k = qk[:, hk:]
        v = v_ref[:, hs]
        li_col = g[:, S_IGATE + h:S_IGATE + h + 1]
        b_col = bcum[:, S_FGATE + h:S_FGATE + h + 1]
        li_row = g_t[S_IGATE + h:S_IGATE + h + 1, :]
        b_row = b_t[S_FGATE + h:S_FGATE + h + 1, :]
        cmat = c_ref[h]
        nvec = n_ref[h:h + 1, :]
        m = m_ref[h:h + 1, 0:1]
        for c in range(rows // L):
            sl = slice(c * L, (c + 1) * L)
            qc, kc, vc = q[sl], k[sl], v[sl]
            bc, lic = b_col[sl], li_col[sl]
            br, lir = b_row[:, sl], li_row[:, sl]
            qcb, kcb, vcb = qc.astype(BF16), kc.astype(BF16), vc.astype(BF16)
            dmat = jnp.where(causal, (bc - br) + lir, -jnp.inf)
            inter = bc + m
            m_t = jnp.maximum(inter, jnp.max(dmat, axis=-1, keepdims=True))
            s_inter = jnp.exp(inter - m_t)
            s = _dot_nt(qcb, kcb) * jnp.exp(dmat - m_t)
            num = _dot(s.astype(BF16), vcb) + s_inter * _dot(qcb, cmat.astype(BF16))
            den = (jnp.sum(s, axis=-1, keepdims=True)
                   + s_inter * jnp.sum(qc * nvec, axis=-1, keepdims=True))
            hc = num / jnp.maximum(jnp.abs(den), jnp.exp(-m_t))
            hn = _rms(hc, gmo_ref[h:h + 1, :]) * _sigmoid(o_ref[sl, hs])
            hm_ref[sl, hs] = hn.astype(BF16)
            b_last = bc[L - 1:L, :]
            g_row = (b_last - br) + lir
            g_col = (b_last - bc) + lic
            m_new = jnp.maximum(b_last + m, jnp.max(g_row, axis=-1, keepdims=True))
            s_old = jnp.exp(b_last + m - m_new)
            kw = kc * jnp.exp(g_col - m_new)
            cmat = s_old * cmat + _dot(kw.T.astype(BF16), vcb)
            nvec = s_old * nvec + jnp.sum(kw, axis=0, keepdims=True)
            m = m_new
        c_ref[h] = cmat
        n_ref[h:h + 1, :] = nvec
        m_ref[h:h + 1, :] = jnp.broadcast_to(m, (1, LANES))

    @pl.when(i == nsteps - 1)
    def _():
        conv_ref[...] = ext_ref[8 + rows - (CONV_W - 1):8 + rows, :]
        cst_ref[...] = c_ref[...]
        nst_ref[...] = n_ref[...]
        mst_ref[...] = m_ref[...]


def _mlstm_prompt(p, ps, bif_row, w_conv, b_conv, wqk, gmo, *, rows):
    t = p.shape[0]
    mw = GROUP
    hv = mw // M_HEADS
    hk = hv // 2
    nsteps = t // rows
    const = lambda shape: pl.BlockSpec(shape, lambda i: (0,) * len(shape))
    return pl.pallas_call(
        functools.partial(_mlstm_prompt_kernel, rows=rows, nsteps=nsteps),
        out_shape=(jax.ShapeDtypeStruct((t, mw), BF16),
                   jax.ShapeDtypeStruct((CONV_W - 1, mw), F32),
                   jax.ShapeDtypeStruct((M_HEADS, hk, hv), F32),
                   jax.ShapeDtypeStruct((8, hk), F32),
                   jax.ShapeDtypeStruct((8, LANES), F32)),
        grid=(nsteps,),
        in_specs=[
            pl.BlockSpec((rows, mw), lambda i: (i, C_UM)),
            pl.BlockSpec((rows, mw), lambda i: (i, C_VM)),
            pl.BlockSpec((rows, mw), lambda i: (i, C_OM)),
            pl.BlockSpec((rows, LANES), lambda i: (i, 0)),
            const((1, LANES)), const((CONV_W, mw)), const((1, mw)),
            const((M_HEADS, hv, 2 * hk)), const((M_HEADS, hv)),
        ],
        out_specs=(pl.BlockSpec((rows, mw), lambda i: (i, 0)),
                   const((CONV_W - 1, mw)), const((M_HEADS, hk, hv)), const((8, hk)), const((8, LANES))),
        scratch_shapes=[pltpu.VMEM((rows + 8, mw), F32), pltpu.VMEM((M_HEADS, hk, hv), F32),
                        pltpu.VMEM((8, hk), F32), pltpu.VMEM((8, LANES), F32)],
        compiler_params=_cparams(("arbitrary",)),
        name="mlstm_prompt",
    )(p, p, p, ps, bif_row, w_conv, b_conv, wqk, gmo)


def _mlstm_sample_kernel(u_ref, v_ref, o_ref, ps_ref, sc_ref, c0_ref, n0_ref, m0_ref,
                         bif_ref, wconv_ref, bconv_ref, wqk_ref, gmo_ref,
                         hm_ref, conv_ref, c1_ref, n1_ref, m1_ref, *, gb):
    mw = u_ref.shape[1]
    hv = mw // M_HEADS
    hk = hv // 2
    u = u_ref[...]
    wc = wconv_ref[...]
    cv = (sc_ref[0] * wc[0:1] + sc_ref[1] * wc[1:2] + sc_ref[2] * wc[2:3] + u * wc[3:4]) + bconv_ref[...]
    conv_ref[0] = sc_ref[1]
    conv_ref[1] = sc_ref[2]
    conv_ref[2] = u
    cb = (cv * _sigmoid(cv)).astype(BF16)
    g = ps_ref[...] + bif_ref[...]
    lf = _log_sigmoid(g)
    for h in range(M_HEADS):
        hs = slice(h * hv, (h + 1) * hv)
        qk = _dot(cb[:, hs], wqk_ref[h])
        q = qk[:, :hk] * (hk ** -0.5)
        k = qk[:, hk:]
        fill = jnp.zeros((LANES - gb, hk), F32)
        q_t = jnp.concatenate([q, fill], axis=0).T
        k_t = jnp.concatenate([k, fill], axis=0).T
        v = v_ref[:, hs]
        for s in range(gb):
            li = g[s:s + 1, S_IGATE + h:S_IGATE + h + 1]
            b = lf[s:s + 1, S_FGATE + h:S_FGATE + h + 1]
            m0 = m0_ref[s:s + 1, h:h + 1]
            qr, kr, vr = q[s:s + 1], k[s:s + 1], v[s:s + 1]
            cmat = c0_ref[s, h]
            nrow = n0_ref[s, h:h + 1, :]
            inter = b + m0
            m_t = jnp.maximum(inter, li)
            s_inter = jnp.exp(inter - m_t)
            dw = jnp.exp(li - m_t)
            sqk = jnp.sum(qr * kr, axis=-1, keepdims=True) * dw
            qc = jnp.sum(q_t[:, s:s + 1] * cmat, axis=0, keepdims=True)
            num = sqk * vr + s_inter * qc
            den = sqk + s_inter * jnp.sum(qr * nrow, axis=-1, keepdims=True)
            hc = num / jnp.maximum(jnp.abs(den), jnp.exp(-m_t))
            hn = _rms(hc, gmo_ref[h:h + 1, :]) * _sigmoid(o_ref[s:s + 1, hs])
            hm_ref[s:s + 1, hs] = hn.astype(BF16)
            c1_ref[s, h] = s_inter * cmat + (dw * k_t[:, s:s + 1]) * vr
            n1_ref[s, h:h + 1, :] = s_inter * nrow + dw * kr
            m1_ref[s:s + 1, h:h + 1] = m_t


def _mlstm_sample(p, ps, conv_t, c0, n0, m0, bif_row, w_conv, b_conv, wqk, gmo, *, gb):
    db = p.shape[0]
    mw = GROUP
    hv = mw // M_HEADS
    hk = hv // 2
    const = lambda shape: pl.BlockSpec(shape, lambda i: (0,) * len(shape))
    return pl.pallas_call(
        functools.partial(_mlstm_sample_kernel, gb=gb),
        out_shape=(jax.ShapeDtypeStruct((db, mw), BF16),
                   jax.ShapeDtypeStruct((CONV_W - 1, db, mw), F32),
                   jax.ShapeDtypeStruct((db, M_HEADS, hk, hv), F32),
                   jax.ShapeDtypeStruct((db, M_HEADS, hk), F32),
                   jax.ShapeDtypeStruct((db, M_HEADS), F32)),
        grid=(db // gb,),
        in_specs=[
            pl.BlockSpec((gb, mw), lambda i: (i, C_UM)),
            pl.BlockSpec((gb, mw), lambda i: (i, C_VM)),
            pl.BlockSpec((gb, mw), lambda i: (i, C_OM)),
            pl.BlockSpec((gb, LANES), lambda i: (i, 0)),
            pl.BlockSpec((CONV_W - 1, gb, mw), lambda i: (0, i, 0)),
            pl.BlockSpec((gb, M_HEADS, hk, hv), lambda i: (i, 0, 0, 0)),
            pl.BlockSpec((gb, M_HEADS, hk), lambda i: (i, 0, 0)),
            pl.BlockSpec((gb, M_HEADS), lambda i: (i, 0)),
            const((1, LANES)), const((CONV_W, mw)), const((1, mw)),
            const((M_HEADS, hv, 2 * hk)), const((M_HEADS, hv)),
        ],
        out_specs=(pl.BlockSpec((gb, mw), lambda i: (i, 0)),
                   pl.BlockSpec((CONV_W - 1, gb, mw), lambda i: (0, i, 0)),
                   pl.BlockSpec((gb, M_HEADS, hk, hv), lambda i: (i, 0, 0, 0)),
                   pl.BlockSpec((gb, M_HEADS, hk), lambda i: (i, 0, 0)),
                   pl.BlockSpec((gb, M_HEADS), lambda i: (i, 0))),
        compiler_params=_cparams(("arbitrary",)),
        name="mlstm_sample",
    )(p, p, p, ps, conv_t, c0, n0, m0, bif_row, w_conv, b_conv, wqk, gmo)


def _dsa_prep_kernel(k_ref, v_ref, ps_ref, gk_ref, k32_ref, v32_ref, kb_ref, vb_ref, ki_ref):
    hd = gk_ref.shape[1]
    for h in range(A_HEADS):
        hs = slice(h * hd, (h + 1) * hd)
        kn = _rms(k_ref[:, hs], gk_ref[...])
        k32_ref[:, hs] = kn
        kb_ref[:, hs] = kn.astype(BF16)
    v = v_ref[...]
    v32_ref[...] = v
    vb_ref[...] = v.astype(BF16)
    ki_ref[...] = ps_ref[...].astype(BF16)


def _dsa_prep(p, ps, gk, *, tm):
    t = p.shape[0]
    hd = gk.shape[1]
    row = lambda w, c: pl.BlockSpec((tm, w), lambda i: (i, c))
    return pl.pallas_call(
        _dsa_prep_kernel,
        out_shape=(jax.ShapeDtypeStruct((t, GROUP), F32), jax.ShapeDtypeStruct((t, GROUP), F32),
                   jax.ShapeDtypeStruct((t, GROUP), BF16), jax.ShapeDtypeStruct((t, GROUP), BF16),
                   jax.ShapeDtypeStruct((t, LANES), BF16)),
        grid=(pl.cdiv(t, tm),),
        in_specs=[row(GROUP, C_KA), row(GROUP, C_VA), row(LANES, 0),
                  pl.BlockSpec((1, hd), lambda i: (0, 0))],
        out_specs=(row(GROUP, 0), row(GROUP, 0), row(GROUP, 0), row(GROUP, 0), row(LANES, 0)),
        compiler_params=_cparams(("arbitrary",)),
        name="dsa_prep",
    )(p, p, ps, gk)


def _dsa_prompt_kernel(q_ref, qi_ref, ps_ref, gq_ref, ki_ref, kb_ref, vb_ref,
                       ha_ref, key_ref, qit_ref, qn_ref, wt_ref, mx_ref, l_ref, acc_ref, *, tq, topk):
    i = pl.program_id(0)
    hd = gq_ref.shape[1]
    ck = tq
    nk = i + 1
    kf = float(topk)
    SUB = 8

    def fold(x, op):
        out = x[0:SUB]
        for r in range(1, ck // SUB):
            out = op(out, x[r * SUB:(r + 1) * SUB])
        return out

    qit_ref[...] = qi_ref[...].T.astype(BF16)
    wt_ref[...] = ps_ref[...].T * ((IDX_HEADS ** -0.5) * (IDX_DIM ** -0.5))
    qscale = (hd ** -0.5) * LOG2E
    for h in range(A_HEADS):
        hs = slice(h * hd, (h + 1) * hd)
        qn_ref[:, hs] = (_rms(q_ref[:, hs], gq_ref[...]) * qscale).astype(BF16)

    q_pos = i * tq + lax.broadcasted_iota(I32, (ck, tq), 1)
    k_in = lax.broadcasted_iota(I32, (ck, tq), 0)

    def score_body(c, carry):
        off = pl.multiple_of(c * ck, ck)
        ki = ki_ref[pl.ds(off, ck), 0:IDX_DIM]
        acc = jnp.zeros((ck, tq), F32)
        for h in range(IDX_HEADS):
            s = _dot(ki, qit_ref[h * IDX_DIM:(h + 1) * IDX_DIM, :])
            acc = acc + jnp.maximum(s, 0.0) * wt_ref[S_WIDX + h:S_WIDX + h + 1, :]
        key_ref[pl.ds(off, ck), :] = jnp.where(k_in + c * ck <= q_pos, _mono_key(acc), INT_MIN)
        return carry

    lax.fori_loop(0, nk, score_body, 0)

    def count_where(pred):
        def one(c, acc):
            off = pl.multiple_of(c * ck, ck)
            return acc + fold(jnp.where(pred(key_ref[pl.ds(off, ck), :], c * ck), 1.0, 0.0), jnp.add)
        acc = lax.fori_loop(0, nk // 2, lambda c2, a: one(2 * c2 + 1, one(2 * c2, a)), jnp.zeros((SUB, tq), F32))
        acc = lax.cond(nk % 2 == 1, lambda a: one(nk - 1, a), lambda a: a, acc)
        return jnp.sum(acc, axis=0, keepdims=True)

    def count_ge(cand):
        return count_where(lambda kc, base: kc >= cand)

    cnt0 = count_ge(jnp.zeros((1, tq), I32))
    ok0 = cnt0 >= kf
    thr0 = jnp.where(ok0, 0, INT_MIN).astype(I32)
    cnt_t0 = jnp.where(ok0, cnt0, kf)

    def bit_body(b, carry):
        thr, cnt_t = carry
        cand = thr + jnp.left_shift(jnp.int32(1), 30 - b)
        cnt = count_ge(cand)
        ok = cnt >= kf
        return jnp.where(ok, cand, thr), jnp.where(ok, cnt, cnt_t)

    thr, cnt_t = lax.fori_loop(0, 31, bit_body, (thr0, cnt_t0))
    has_tie = jnp.max(cnt_t) > kf
    bias_on, bias_off = 0.0, -jnp.inf

    @pl.when(jnp.logical_not(has_tie))
    def _():
        tmin = jnp.maximum(thr, INT_MIN + 1)

        def body(c, carry):
            off = pl.multiple_of(c * ck, ck)
            bias = jnp.where(key_ref[pl.ds(off, ck), :] >= tmin, bias_on, bias_off)
            key_ref[pl.ds(off, ck), :] = lax.bitcast_convert_type(bias.T, I32)
            return carry
        lax.fori_loop(0, nk, body, 0)

    @pl.when(has_tie)
    def _():
        n_gt = count_where(lambda kc, base: kc > thr)
        need = kf - n_gt

        def pos_body(b, x):
            cand = x + jnp.left_shift(jnp.int32(1), b)
            cnt = count_where(lambda kc, base: (kc == thr) & (k_in + base < cand))
            return jnp.where(cnt < need, cand, x)

        nbits = int(math.ceil(math.log2(key_ref.shape[0]))) + 1
        x = lax.fori_loop(0, nbits, lambda b, x: pos_body(nbits - 1 - b, x), jnp.zeros((1, tq), I32))
        x = jnp.where(thr > INT_MIN, x, -1)

        def body(c, carry):
            off = pl.multiple_of(c * ck, ck)
            kc = key_ref[pl.ds(off, ck), :]
            sel = (kc > thr) | ((kc == thr) & (k_in + c * ck <= x))
            key_ref[pl.ds(off, ck), :] = lax.bitcast_convert_type(jnp.where(sel, bias_on, bias_off).T, I32)
            return carry
        lax.fori_loop(0, nk, body, 0)

    ncol = ck // LANES

    def lane_fold(x, op):
        out = x[:, 0:LANES]
        for r in range(1, ncol):
            out = op(out, x[:, r * LANES:(r + 1) * LANES])
        return out

    def scores(c, h):
        off = pl.multiple_of(c * ck, ck)
        hs = slice(h * hd, (h + 1) * hd)
        bias = lax.bitcast_convert_type(key_ref[pl.ds(off, ck), :], F32)
        return _dot_nt(qn_ref[:, hs], kb_ref[pl.ds(off, ck), hs]) + bias

    mx_ref[...] = jnp.full(mx_ref.shape, -jnp.inf, F32)

    def max_body(c, carry):
        for h in range(A_HEADS):
            mx_ref[h] = jnp.maximum(mx_ref[h], lane_fold(scores(c, h), jnp.maximum))
        return carry

    lax.fori_loop(0, nk, max_body, 0)
    for h in range(A_HEADS):
        mx_ref[h] = jnp.broadcast_to(jnp.max(mx_ref[h], axis=-1, keepdims=True), (tq, LANES))

    acc_ref[...] = jnp.zeros(acc_ref.shape, F32)
    l_ref[...] = jnp.zeros(l_ref.shape, F32)

    def sum_body(c, carry):
        off = pl.multiple_of(c * ck, ck)
        for h in range(A_HEADS):
            hs = slice(h * hd, (h + 1) * hd)
            p = jnp.exp2(scores(c, h) - jnp.tile(mx_ref[h], (1, ncol)))
            l_ref[h] += lane_fold(p, jnp.add)
            acc_ref[h] += _dot(p.astype(BF16), vb_ref[pl.ds(off, ck), hs])
        return carry

    lax.fori_loop(0, nk, sum_body, 0)
    for h in range(A_HEADS):
        l_col = jnp.sum(l_ref[h], axis=-1, keepdims=True)
        ha_ref[:, h * hd:(h + 1) * hd] = (acc_ref[h] / l_col).astype(BF16)


def _dsa_prompt(p, ps, gq, ki, kb, vb, *, tq):
    t = p.shape[0]
    hd = gq.shape[1]
    topk = min(TOPK_MAX, t // 4)
    return pl.pallas_call(
        functools.partial(_dsa_prompt_kernel, tq=tq, topk=topk),
        out_shape=jax.ShapeDtypeStruct((t, GROUP), BF16),
        grid=(t // tq,),
        in_specs=[
            pl.BlockSpec((tq, GROUP), lambda i: (i, C_QA)),
            pl.BlockSpec((tq, GROUP), lambda i: (i, C_QI)),
            pl.BlockSpec((tq, LANES), lambda i: (i, 0)),
            pl.BlockSpec((1, hd), lambda i: (0, 0)),
            _vmem_spec(), _vmem_spec(), _vmem_spec(),
        ],
        out_specs=pl.BlockSpec((tq, GROUP), lambda i: (i, 0)),
        scratch_shapes=[pltpu.VMEM((t, tq), I32),
                        pltpu.VMEM((IDX_HEADS * IDX_DIM, tq), BF16),
                        pltpu.VMEM((tq, GROUP), BF16),
                        pltpu.VMEM((LANES, tq), F32),
                        pltpu.VMEM((A_HEADS, tq, LANES), F32),
                        pltpu.VMEM((A_HEADS, tq, LANES), F32),
                        pltpu.VMEM((A_HEADS, tq, hd), F32)],
        compiler_params=_cparams(("arbitrary",)),
        name="dsa_prompt",
    )(p, p, ps, gq, ki, kb, vb)


PAGES_PER_STEP = 16


def _dsa_sample_prep_kernel(q_ref, k_ref, v_ref, qi_ref, ps_ref, gq_ref, gk_ref,
                            qs_ref, kn_ref, vn_ref, sn_ref):
    hd = gq_ref.shape[1]
    qscale = (hd ** -0.5) * LOG2E
    for h in range(A_HEADS):
        hs = slice(h * hd, (h + 1) * hd)
        qs_ref[:, hs] = _rms(q_ref[:, hs], gq_ref[...]) * qscale
        kn_ref[:, hs] = _rms(k_ref[:, hs], gk_ref[...])
    vn_ref[...] = v_ref[...]
    ps = ps_ref[...]
    lane = lax.broadcasted_iota(I32, ps.shape, 1)
    ke = jnp.where(lane < IDX_DIM, ps, 0.0)
    ko = pltpu.roll(ke, IDX_DIM, axis=1)
    w_scale = (IDX_HEADS ** -0.5) * (IDX_DIM ** -0.5)
    acc = jnp.zeros((ps.shape[0], 1), F32)
    for pr in range(IDX_HEADS // 2):
        q2 = qi_ref[:, pr * LANES:(pr + 1) * LANES]
        se = jnp.sum(q2 * ke, axis=-1, keepdims=True)
        so = jnp.sum(q2 * ko, axis=-1, keepdims=True)
        acc = acc + jnp.maximum(se, 0.0) * (ps[:, S_WIDX + 2 * pr:S_WIDX + 2 * pr + 1] * w_scale)
        acc = acc + jnp.maximum(so, 0.0) * (ps[:, S_WIDX + 2 * pr + 1:S_WIDX + 2 * pr + 2] * w_scale)
    sn_ref[...] = jnp.broadcast_to(acc, sn_ref.shape)


def _dsa_sample_prep(p, ps, gq, gk):
    db = p.shape[0]
    hd = gq.shape[1]
    blk = lambda c: pl.BlockSpec((db, GROUP), lambda i: (0, c))
    full = lambda shape: pl.BlockSpec(shape, lambda i: (0,) * len(shape))
    return pl.pallas_call(
        _dsa_sample_prep_kernel,
        out_shape=(jax.ShapeDtypeStruct((db, GROUP), F32), jax.ShapeDtypeStruct((db, GROUP), F32),
                   jax.ShapeDtypeStruct((db, GROUP), F32), jax.ShapeDtypeStruct((db, LANES), F32)),
        grid=(1,),
        in_specs=[blk(C_QA), blk(C_KA), blk(C_VA), blk(C_QI), full((db, LANES)), full((1, hd)), full((1, hd))],
        out_specs=(full((db, GROUP)), full((db, GROUP)), full((db, GROUP)), full((db, LANES))),
        compiler_params=_cparams(("arbitrary",)),
        name="dsa_sample_prep",
    )(p, p, p, p, ps, gq, gk)


def _dsa_sample_scores_kernel(pt_ref, qi_ref, w_ref, *refs):
    page_refs, out_ref = refs[:PAGES_PER_STEP], refs[PAGES_PER_STEP]
    qi = qi_ref[0].astype(BF16)
    w = w_ref[0]
    for r in range(PAGES_PER_STEP):
        ki = page_refs[r][0, 0].astype(BF16)
        s = jnp.maximum(_dot_nt(qi, ki), 0.0) * w
        out_ref[0, :, r * PAGE_SIZE:(r + 1) * PAGE_SIZE] = jnp.sum(s, axis=0, keepdims=True)


def _dsa_sample_scores(page_table, qi3, w3, cache_kidx):
    db, n_pages = page_table.shape
    steps = n_pages // PAGES_PER_STEP

    def page_spec(r):
        return pl.BlockSpec((1, 1, PAGE_SIZE, IDX_DIM),
                            lambda b, j, pt: (0, pt[b, j * PAGES_PER_STEP + r], 0, 0))

    grid_spec = pltpu.PrefetchScalarGridSpec(
        num_scalar_prefetch=1,
        grid=(db, steps),
        in_specs=[pl.BlockSpec((1, IDX_HEADS, IDX_DIM), lambda b, j, pt: (b, 0, 0)),
                  pl.BlockSpec((1, IDX_HEADS, 1), lambda b, j, pt: (b, 0, 0))]
                 + [page_spec(r) for r in range(PAGES_PER_STEP)],
        out_specs=pl.BlockSpec((1, 1, PAGES_PER_STEP * PAGE_SIZE), lambda b, j, pt: (b * steps + j, 0, 0)),
    )
    return pl.pallas_call(
        _dsa_sample_scores_kernel,
        out_shape=jax.ShapeDtypeStruct((db * steps, 1, PAGES_PER_STEP * PAGE_SIZE), F32),
        grid_spec=grid_spec,
        compiler_params=_cparams(("arbitrary", "arbitrary")),
        name="dsa_sample_scores",
    )(page_table, qi3, w3, *([cache_kidx] * PAGES_PER_STEP))


def _dsa_sample_select_kernel(sc_ref, sn_ref, idx_ref, nv_ref, ns_ref, *, topk):
    db, npg, pgs = sc_ref.shape
    kf = float(topk)
    key = _mono_key(sc_ref[...])
    key_n = _mono_key(sn_ref[...][:, 0:1]).reshape(db, 1, 1)

    def total(x):
        return jnp.sum(jnp.sum(x, axis=2, keepdims=True), axis=1, keepdims=True)

    def count_ge(cand):
        return (total(jnp.where(key >= cand, 1.0, 0.0)) + jnp.where(key_n >= cand, 1.0, 0.0))

    cnt0 = count_ge(jnp.zeros((db, 1, 1), I32))
    thr0 = jnp.where(cnt0 >= kf, 0, INT_MIN).astype(I32)

    def bit_body(b, thr):
        cand = thr + jnp.left_shift(jnp.int32(1), 30 - b)
        return jnp.where(count_ge(cand) >= kf, cand, thr)

    thr = lax.fori_loop(0, 31, bit_body, thr0)
    gt = jnp.where(key > thr, 1.0, 0.0)
    eq = jnp.where(key == thr, 1.0, 0.0)
    gt_n = jnp.where(key_n > thr, 1.0, 0.0)
    eq_n = jnp.where(key_n == thr, 1.0, 0.0)
    need = kf - (total(gt) + gt_n)

    r_i = lax.broadcasted_iota(I32, (pgs, pgs), 0)
    c_i = lax.broadcasted_iota(I32, (pgs, pgs), 1)
    upper = jnp.where(r_i <= c_i, 1.0, 0.0).astype(BF16)
    lower = jnp.where(c_i <= r_i, 1.0, 0.0).astype(BF16)
    lower_pg = jnp.where(lax.broadcasted_iota(I32, (npg, npg), 1)
                         <= lax.broadcasted_iota(I32, (npg, npg), 0), 1.0, 0.0).astype(BF16)
    j_row = lax.broadcasted_iota(I32, (1, topk), 1).astype(F32)
    p_col = lax.broadcasted_iota(I32, (npg, 1), 0).astype(F32)

    def page_prefix(x):
        pin = _dot(x.astype(BF16), upper)
        rt = pin[:, pgs - 1:pgs]
        incl = _dot(lower_pg, jnp.broadcast_to(rt, (npg, pgs)).astype(BF16))[:, 0:1]
        return pin, rt, incl

    for b in range(db):
        eq_b = eq[b]
        pin_e, rt_e, incl_e = page_prefix(eq_b)
        rank_e = (incl_e - rt_e) + pin_e
        need_b = need[b]
        sel = gt[b] + jnp.where(rank_e <= need_b, eq_b, 0.0)
        n_eq = incl_e[npg - 1:npg, :]
        new_sel = gt_n[b] + jnp.where(n_eq + 1.0 <= need_b, eq_n[b], 0.0)

        pin, rt, incl = page_prefix(sel)
        ex = incl - rt
        n_sel = incl[npg - 1:npg, :]
        page_j = jnp.sum(jnp.where(incl <= j_row, 1.0, 0.0), axis=0, keepdims=True)
        oh_t = p_col == page_j
        ex_j = jnp.sum(jnp.where(oh_t, ex, 0.0), axis=0, keepdims=True)
        r_j = j_row - ex_j + 1.0
        pin_t = _dot(lower, sel.T.astype(BF16))
        rows_t = _dot(pin_t.astype(BF16), jnp.where(oh_t, 1.0, 0.0).astype(BF16))
        off_j = jnp.sum(jnp.where(rows_t < r_j, 1.0, 0.0), axis=0, keepdims=True)
        idx = jnp.where(j_row < n_sel, page_j * float(pgs) + off_j, 0.0)
        idx_ref[b:b + 1, :] = idx.astype(I32)
        nv_ref[b:b + 1, :] = jnp.broadcast_to(n_sel, (1, LANES)).astype(I32)
        ns_ref[b:b + 1, :] = jnp.broadcast_to(new_sel, (1, LANES)).astype(I32)


def _dsa_sample_select(sc3, sn, *, topk):
    db = sc3.shape[0]
    return pl.pallas_call(
        functools.partial(_dsa_sample_select_kernel, topk=topk),
        out_shape=(jax.ShapeDtypeStruct((db, topk), I32), jax.ShapeDtypeStruct((db, LANES), I32),
                   jax.ShapeDtypeStruct((db, LANES), I32)),
        in_specs=[_vmem_spec(), _vmem_spec()],
        out_specs=(_vmem_spec(), _vmem_spec(), _vmem_spec()),
        compiler_params=_cparams(None),
        name="dsa_sample_select",
    )(sc3, sn)


def _dsa_sample_attend_kernel(idx_ref, pt_ref, nv_ref, ns_ref, q_ref, kn_ref, vn_ref, ck_ref, cv_ref,
                              o_ref, kbuf, vbuf, sem, *, topk, nseq):
    b = pl.program_id(0)
    slot = b % 2
    nh = A_HEADS
    hd = q_ref.shape[2] // nh

    def row_copies(seq, j, sl):
        r = idx_ref[seq, j]
        pg = pt_ref[seq, lax.shift_right_logical(r, 7)]
        off = r & (PAGE_SIZE - 1)
        dst = pl.ds(pl.multiple_of(j * nh, nh), nh)
        return (pltpu.make_async_copy(ck_ref.at[0, pg, off], kbuf.at[sl, dst], sem.at[0, sl]),
                pltpu.make_async_copy(cv_ref.at[0, pg, off], vbuf.at[sl, dst], sem.at[1, sl]))

    def issue(seq, sl):
        def body(j, carry):
            ck, cv = row_copies(seq, j, sl)
            ck.start()
            cv.start()
            return carry
        lax.fori_loop(0, topk, body, 0)

    @pl.when(b == 0)
    def _():
        issue(0, 0)

    @pl.when(b + 1 < nseq)
    def _():
        issue(b + 1, 1 - slot)

    def wait_body(j, carry):
        ck, cv = row_copies(b, j, slot)
        ck.wait()
        cv.wait()
        return carry
    lax.fori_loop(0, topk, wait_body, 0)

    n_valid = nv_ref[b]
    new_on = ns_ref[b] > 0
    j_row = lax.broadcasted_iota(I32, (1, topk), 1)
    for h in range(nh):
        hs = slice(h * hd, (h + 1) * hd)
        qh = q_ref[0, :, hs]
        kh = kbuf[slot, pl.ds(h, topk, stride=nh), :]
        vh = vbuf[slot, pl.ds(h, topk, stride=nh), :]
        s = _dot_nt(qh.astype(BF16), kh.astype(BF16))
        s = jnp.where(j_row < n_valid, s, -jnp.inf)
        s_new = jnp.sum(qh * kn_ref[0, :, hs], axis=-1, keepdims=True)
        s_new = jnp.where(new_on, s_new, -jnp.inf)
        m = jnp.maximum(jnp.max(s, axis=-1, keepdims=True), s_new)
        p = jnp.exp2(s - m)
        p_new = jnp.exp2(s_new - m)
        l = jnp.sum(p, axis=-1, keepdims=True) + p_new
        o = _dot(p.astype(BF16), vh.astype(BF16)) + p_new * vn_ref[0, :, hs]
        o_ref[0, :, hs] = o / l


def _dsa_sample_attend(idx, page_table, nv, ns, qs3, kn3, vn3, cache_k, cache_v, *, topk):
    db = idx.shape[0]
    w = qs3.shape[2]
    hd = w // A_HEADS
    row = pl.BlockSpec((1, 1, w), lambda b, *_: (b, 0, 0))
    grid_spec = pltpu.PrefetchScalarGridSpec(
        num_scalar_prefetch=4,
        grid=(db,),
        in_specs=[row, row, row, pl.BlockSpec(memory_space=pl.ANY), pl.BlockSpec(memory_space=pl.ANY)],
        out_specs=row,
        scratch_shapes=[pltpu.VMEM((2, topk * A_HEADS, hd), F32), pltpu.VMEM((2, topk * A_HEADS, hd), F32),
                        pltpu.SemaphoreType.DMA((2, 2))],
    )
    return pl.pallas_call(
        functools.partial(_dsa_sample_attend_kernel, topk=topk, nseq=db),
        out_shape=jax.ShapeDtypeStruct((db, 1, w), F32),
        grid_spec=grid_spec,
        compiler_params=_cparams(("arbitrary",)),
        name="dsa_sample_attend",
    )(idx, page_table, nv, ns, qs3, kn3, vn3, cache_k, cache_v)


def _merge_kernel(hm_ref, ha_ref, gm_ref, ga_ref, x_ref, wmo_ref, wao_ref, wout_ref, y_ref):
    t1 = _dot(hm_ref[...].astype(BF16), wmo_ref[...])
    t2 = _dot(ha_ref[...].astype(BF16), wao_ref[...])
    z = _sigmoid(gm_ref[...]) * t1 + _sigmoid(ga_ref[...]) * t2
    y_ref[...] = x_ref[...] + _dot(z.astype(BF16), wout_ref[...])


def _merge(hm, ha, p, x, wmo, wao, wout, *, tm):
    m, d = x.shape
    w = hm.shape[1]
    return pl.pallas_call(
        _merge_kernel,
        out_shape=jax.ShapeDtypeStruct((m, d), F32),
        grid=(pl.cdiv(m, tm),),
        in_specs=[
            pl.BlockSpec((tm, w), lambda i: (i, 0)),
            pl.BlockSpec((tm, w), lambda i: (i, 0)),
            pl.BlockSpec((tm, d), lambda i: (i, 0)),
            pl.BlockSpec((tm, d), lambda i: (i, 1)),
            pl.BlockSpec((tm, d), lambda i: (i, 0)),
            _vmem_spec(), _vmem_spec(), _vmem_spec(),
        ],
        out_specs=pl.BlockSpec((tm, d), lambda i: (i, 0)),
        compiler_params=_cparams(("arbitrary",)),
        name="merge",
    )(hm, ha, p, p, x, wmo, wao, wout)


FFN_TM = 512
FFN_TF = 512
PROJ_TM = 512
MERGE_TM = 256
MLSTM_ROWS = 256
MLSTM_SAMPLE_GROUP = 8
DSA_TQ = 256
DSA_PREP_TM = 512


def kernel(x_prompt, x_sample, cache_k, cache_v, cache_kidx, state_conv, state_C, state_n, state_m, page_table, norm_ffn1, w_ffn1_up, w_ffn1_down, norm_mix, w_in, b_if, w_conv, b_conv, w_qk_m, norm_mlstm_out, norm_q, norm_k, w_mo, w_ao, w_out, norm_ffn2, w_ffn2_up, w_ffn2_down):
    depth = w_in.shape[0]
    assert depth == 1, "single-layer step"
    bsz, seq, d = x_prompt.shape
    assert bsz == 1
    db, dseq, _ = x_sample.shape
    assert dseq == 1
    n_pages = page_table.shape[1]
    assert cache_k.shape[2] == PAGE_SIZE and n_pages % PAGES_PER_STEP == 0
    past = n_pages * PAGE_SIZE
    topk_s = min(TOPK_MAX, (past + dseq) // 4)
    l = 0

    xp = x_prompt.reshape(seq, d)
    xs = x_sample.reshape(db, d)

    wu1, wd1 = w_ffn1_up[l].astype(BF16), w_ffn1_down[l].astype(BF16)
    wu2, wd2 = w_ffn2_up[l].astype(BF16), w_ffn2_down[l].astype(BF16)
    w_pad, w_small, tab = _prep_w_in(w_in[l], d)
    wmo, wao, wout = w_mo[l].astype(BF16), w_ao[l].astype(BF16), w_out[l].astype(BF16)
    wqk = w_qk_m[l].astype(BF16)
    g1, gmix, g2 = norm_ffn1[l][None], norm_mix[l][None], norm_ffn2[l][None]
    gq, gk = norm_q[l][None], norm_k[l][None]
    gmo = norm_mlstm_out[l]
    bif_row = jnp.zeros((1, LANES), F32).at[0, S_IGATE:S_IGATE + 2 * M_HEADS].set(b_if[l])
    wcv, bcv = w_conv[l], b_conv[l][None]

    xp1 = _ffn(xp, g1, wu1, wd1, tm=FFN_TM, tf=FFN_TF)
    xs1 = _ffn(xs, g1, wu1, wd1, tm=db, tf=FFN_TF)

    pp, pps = _inproj(xp1, gmix, w_pad, w_small, tab, tm=PROJ_TM)
    sp, sps = _inproj(xs1, gmix, w_pad, w_small, tab, tm=db)

    hm_p, conv_p, c_p, n_p, m_p = _mlstm_prompt(pp, pps, bif_row, wcv, bcv, wqk, gmo, rows=MLSTM_ROWS)
    k32, v32, kb, vb, ki = _dsa_prep(pp, pps, gk, tm=DSA_PREP_TM)
    ha_p = _dsa_prompt(pp, pps, gq, ki, kb, vb, tq=DSA_TQ)
    yp = _merge(hm_p, ha_p, pp, xp1, wmo, wao, wout, tm=MERGE_TM)
    out_p = _ffn(yp, g2, wu2, wd2, tm=FFN_TM, tf=FFN_TF)

    conv_t = jnp.transpose(state_conv[l], (1, 0, 2))
    hm_s, conv_s, c_s, n_s, m_s = _mlstm_sample(sp, sps, conv_t, state_C[l], state_n[l], state_m[l],
                                                bif_row, wcv, bcv, wqk, gmo, gb=MLSTM_SAMPLE_GROUP)
    qs, kn_s, vn_s, sn = _dsa_sample_prep(sp, sps, gq, gk)
    qi3 = sp[:, C_QI * GROUP:(C_QI + 1) * GROUP].reshape(db, IDX_HEADS, IDX_DIM)
    w3 = (sps[:, S_WIDX:S_WIDX + IDX_HEADS] * ((IDX_HEADS ** -0.5) * (IDX_DIM ** -0.5))).reshape(db, IDX_HEADS, 1)
    sc = _dsa_sample_scores(page_table, qi3, w3, cache_kidx)
    sc3 = sc.reshape(db, n_pages, PAGE_SIZE)
    idx, nv, ns = _dsa_sample_select(sc3, sn, topk=topk_s)
    ha_s = _dsa_sample_attend(idx, page_table, nv[:, 0], ns[:, 0], qs.reshape(db, 1, GROUP),
                              kn_s.reshape(db, 1, GROUP), vn_s.reshape(db, 1, GROUP),
                              cache_k, cache_v, topk=topk_s)
    ys = _merge(hm_s, ha_s.reshape(db, GROUP), sp, xs1, wmo, wao, wout, tm=db)
    out_s = _ffn(ys, g2, wu2, wd2, tm=db, tf=FFN_TF)

    hd = GROUP // A_HEADS
    hv = GROUP // M_HEADS
    hk = hv // 2
    return (
        out_p.reshape(bsz, seq, d),
        out_s.reshape(db, dseq, d),
        k32.reshape(depth, bsz, seq, A_HEADS, hd),
        v32.reshape(depth, bsz, seq, A_HEADS, hd),
        pps[:, S_KIDX:S_KIDX + IDX_DIM].reshape(depth, bsz, seq, IDX_DIM),
        conv_p.reshape(depth, bsz, CONV_W - 1, GROUP),
        c_p.reshape(depth, bsz, M_HEADS, hk, hv),
        n_p[:M_HEADS].reshape(depth, bsz, M_HEADS, hk),
        m_p[:M_HEADS, 0].reshape(depth, bsz, M_HEADS),
        kn_s.reshape(depth, db, dseq, A_HEADS, hd),
        vn_s.reshape(depth, db, dseq, A_HEADS, hd),
        sps[:, S_KIDX:S_KIDX + IDX_DIM].reshape(depth, db, dseq, IDX_DIM),
        jnp.transpose(conv_s, (1, 0, 2)).reshape(depth, db, CONV_W - 1, GROUP),
        c_s.reshape(depth, db, M_HEADS, hk, hv),
        n_s.reshape(depth, db, M_HEADS, hk),
        m_s.reshape(depth, db, M_HEADS),
    )
```

```python
import functools
import math

import jax
import jax.numpy as jnp
from jax import lax
from jax.experimental import pallas as pl
from jax.experimental.pallas import tpu as pltpu

F32 = jnp.float32
BF16 = jnp.bfloat16
I32 = jnp.int32

EPS = 1e-6
LANES = 128
PAGE_SIZE = 128
M_HEADS = 4
M_CHUNK = 64
CONV_W = 4
A_HEADS = 8
IDX_HEADS = 16
IDX_DIM = 64
TOPK_MAX = 256
N_BRANCHES = 2
GROUP = 1024
INT_MIN = -2 ** 31
NEG_INIT = -1e30
LOG2E = 1.4426950408889634
SOFTMAX_SUM_FLOOR = 2.0 ** -100
VMEM_LIMIT = 56 * 1024 * 1024

S_KIDX = 0
S_WIDX = IDX_DIM
S_IGATE = IDX_DIM + IDX_HEADS
S_FGATE = S_IGATE + M_HEADS

C_GATE, C_UM, C_VM, C_OM, C_QA, C_KA, C_VA, C_QI = 0, 4, 5, 6, 7, 8, 9, 10
N_BIG_BLOCKS = 11


def _cparams(sem, vmem=VMEM_LIMIT):
    return pltpu.CompilerParams(dimension_semantics=sem, vmem_limit_bytes=vmem)


def _vmem_spec():
    return pl.BlockSpec(memory_space=pltpu.VMEM)


def _rms(x, g):
    return x * lax.rsqrt(jnp.mean(x * x, axis=-1, keepdims=True) + EPS) * g


def _sigmoid(x):
    return 1.0 / (1.0 + jnp.exp(-x))


def _log_sigmoid(x):
    return jnp.minimum(x, 0.0) - jnp.log(1.0 + jnp.exp(-jnp.abs(x)))


def _dot(a, b):
    return jnp.dot(a, b, preferred_element_type=F32)


def _dot_nt(a, b):
    return lax.dot_general(a, b, (((1,), (1,)), ((), ())), preferred_element_type=F32)


def _dot_exact(a, b):
    return jnp.dot(a, b, preferred_element_type=F32, precision=lax.Precision.HIGHEST)


def _mono_key(x):
    b = lax.bitcast_convert_type(x, I32)
    return b ^ ((b >> 31) & jnp.int32(0x7FFFFFFF))


def _ffn_kernel(x_ref, g_ref, wa_ref, wb_ref, wd_ref, wat_ref, wbt_ref, wdt_ref, o_ref, h_ref, *, nfull):
    j = pl.program_id(1)

    @pl.when(j == 0)
    def _():
        h_ref[...] = _rms(x_ref[...], g_ref[...]).astype(BF16)

    def half_step(wa, wb, wd):
        h = h_ref[...]
        a = _dot(h, wa[...])
        act = (a * _sigmoid(a) * _dot(h, wb[...])).astype(BF16)
        return _dot(act, wd[...])

    @pl.when(j == 0)
    def _():
        o_ref[...] = half_step(wa_ref, wb_ref, wd_ref)

    @pl.when((j > 0) & (j < nfull))
    def _():
        o_ref[...] += half_step(wa_ref, wb_ref, wd_ref)

    @pl.when(j == nfull)
    def _():
        o_ref[...] = x_ref[...] + 0.5 * (o_ref[...] + half_step(wat_ref, wbt_ref, wdt_ref))


def _ffn(x, g, w_up, w_down, *, tm, tf):
    m, d = x.shape
    d_ff = w_down.shape[0]
    nfull = d_ff // tf
    tail = d_ff - nfull * tf
    assert tail > 0 and tail % LANES == 0 and d_ff % LANES == 0
    elem = lambda r, c: (pl.Element(r), pl.Element(c))
    main = lambda j: jnp.minimum(j, nfull - 1) * tf
    main_b = lambda j: (d_ff // LANES + jnp.minimum(j, nfull - 1) * (tf // LANES)) * LANES
    tail_b = (d_ff + nfull * tf) // LANES
    return pl.pallas_call(
        functools.partial(_ffn_kernel, nfull=nfull),
        out_shape=jax.ShapeDtypeStruct((m, d), F32),
        grid=(pl.cdiv(m, tm), nfull + 1),
        in_specs=[
            pl.BlockSpec((tm, d), lambda i, j: (i, 0)),
            pl.BlockSpec((1, d), lambda i, j: (0, 0)),
            pl.BlockSpec(elem(d, tf), lambda i, j: (0, main(j))),
            pl.BlockSpec(elem(d, tf), lambda i, j: (0, main_b(j))),
            pl.BlockSpec(elem(tf, d), lambda i, j: (main(j), 0)),
            pl.BlockSpec(elem(d, tail), lambda i, j: (0, nfull * tf)),
            pl.BlockSpec(elem(d, tail), lambda i, j: (0, tail_b * LANES)),
            pl.BlockSpec(elem(tail, d), lambda i, j: (nfull * tf, 0)),
        ],
        out_specs=pl.BlockSpec((tm, d), lambda i, j: (i, 0)),
        scratch_shapes=[pltpu.VMEM((tm, d), BF16)],
        compiler_params=_cparams(("arbitrary", "arbitrary")),
        name="ffn",
    )(x, g, w_up, w_up, w_down, w_up, w_up, w_down)


INPROJ_WINDOW = GROUP + LANES


def _inproj_kernel(tab_ref, x_ref, g_ref, w_ref, ws_ref, p_ref, ps_ref, h_ref):
    j = pl.program_id(1)

    @pl.when(j == 0)
    def _():
        h = _rms(x_ref[...], g_ref[...]).astype(BF16)
        h_ref[...] = h
        ps_ref[...] = _dot(h, ws_ref[...])

    res = _dot(h_ref[...], w_ref[...])
    p_ref[...] = pltpu.roll(res, tab_ref[1, j], axis=1)[:, :GROUP]


def _inproj(x, g, w_pad, w_small, tab, *, tm):
    m, d = x.shape
    n_groups = tab.shape[1]
    grid_spec = pltpu.PrefetchScalarGridSpec(
        num_scalar_prefetch=1,
        grid=(pl.cdiv(m, tm), n_groups),
        in_specs=[
            pl.BlockSpec((tm, d), lambda i, j, tab: (i, 0)),
            pl.BlockSpec((1, d), lambda i, j, tab: (0, 0)),
            pl.BlockSpec((pl.Element(d), pl.Element(INPROJ_WINDOW)), lambda i, j, tab: (0, tab[0, j] * LANES)),
            pl.BlockSpec((d, LANES), lambda i, j, tab: (0, 0)),
        ],
        out_specs=(pl.BlockSpec((tm, GROUP), lambda i, j, tab: (i, j)),
                   pl.BlockSpec((tm, LANES), lambda i, j, tab: (i, 0))),
        scratch_shapes=[pltpu.VMEM((tm, d), BF16)],
    )
    return pl.pallas_call(
        _inproj_kernel,
        out_shape=(jax.ShapeDtypeStruct((m, n_groups * GROUP), F32), jax.ShapeDtypeStruct((m, LANES), F32)),
        grid_spec=grid_spec,
        compiler_params=_cparams(("arbitrary", "arbitrary")),
        name="inproj",
    )(tab, x, g, w_pad, w_small)


def _prep_w_in(w_in, d_model):
    mw = GROUP
    sizes = (mw, mw, mw, M_HEADS, M_HEADS, GROUP, GROUP, GROUP, IDX_HEADS * IDX_DIM, IDX_DIM,
             IDX_HEADS, N_BRANCHES * d_model)
    offs = [0]
    for s in sizes:
        offs.append(offs[-1] + s)
    u_m, v_m, o_m, i_m, f_m, q_a, k_a, v_a, q_i, k_i, w_i, gate = range(12)
    starts = [offs[gate] + r * GROUP for r in range(N_BRANCHES * d_model // GROUP)]
    starts += [offs[k] for k in (u_m, v_m, o_m, q_a, k_a, v_a, q_i)]
    assert len(starts) == N_BIG_BLOCKS
    aligned = [s // LANES * LANES for s in starts]
    rolls = [(INPROJ_WINDOW - (s - a)) % INPROJ_WINDOW for s, a in zip(starts, aligned)]
    tab = jnp.array([[a // LANES for a in aligned], rolls], I32)
    width = max(aligned) + INPROJ_WINDOW
    w16 = w_in.astype(BF16)
    w_pad = jnp.pad(w16, ((0, 0), (0, width - w_in.shape[1])))
    col = lambda k: w16[:, offs[k]:offs[k + 1]]
    fill = jnp.zeros((w_in.shape[0], LANES - S_FGATE - M_HEADS), BF16)
    w_small = jnp.concatenate([col(k_i), col(w_i), col(i_m), col(f_m), fill], axis=1)
    return w_pad, w_small, tab


def _mlstm_prompt_kernel(u_ref, v_ref, o_ref, ps_ref, bif_ref, wconv_ref, bconv_ref, wqk_ref, gmo_ref,
                         hm_ref, conv_ref, cst_ref, nst_ref, mst_ref,
                         ext_ref, c_ref, n_ref, m_ref, *, rows, nsteps):
    i = pl.program_id(0)
    mw = u_ref.shape[1]
    hv = mw // M_HEADS
    hk = hv // 2
    L = M_CHUNK

    @pl.when(i == 0)
    def _():
        ext_ref[0:8, :] = jnp.zeros((8, mw), F32)
        c_ref[...] = jnp.zeros(c_ref.shape, F32)
        n_ref[...] = jnp.zeros(n_ref.shape, F32)
        m_ref[...] = jnp.zeros(m_ref.shape, F32)

    u = u_ref[...]
    ext_ref[8:8 + rows, :] = u
    wc = wconv_ref[...]
    cv = (ext_ref[5:5 + rows, :] * wc[0:1] + ext_ref[6:6 + rows, :] * wc[1:2]
          + ext_ref[7:7 + rows, :] * wc[2:3] + u * wc[3:4]) + bconv_ref[...]
    ext_ref[0:8, :] = u[rows - 8:rows, :]
    cb = (cv * _sigmoid(cv)).astype(BF16)

    g = ps_ref[...] + bif_ref[...]
    lf = _log_sigmoid(g)
    r_i = lax.broadcasted_iota(I32, (rows, rows), 0)
    c_i = lax.broadcasted_iota(I32, (rows, rows), 1)
    tri = jnp.where((r_i // L == c_i // L) & (c_i <= r_i), 1.0, 0.0).astype(F32)
    bcum = _dot_exact(tri, lf)
    g_t = g.T
    b_t = bcum.T
    causal = lax.broadcasted_iota(I32, (L, L), 1) <= lax.broadcasted_iota(I32, (L, L), 0)

    for h in range(M_HEADS):
        hs = slice(h * hv, (h + 1) * hv)
        qk = _dot(cb[:, hs], wqk_ref[h])
        q = qk[:, :hk] * (hk ** -0.5)
        k = qk[:, hk:]
        v = v_ref[:, hs]
        li_col = g[:, S_IGATE + h:S_IGATE + h + 1]
        b_col = bcum[:, S_FGATE + h:S_FGATE + h + 1]
        li_row = g_t[S_IGATE + h:S_IGATE + h + 1, :]
        b_row = b_t[S_FGATE + h:S_FGATE + h + 1, :]
        cmat = c_ref[h]
        nvec = n_ref[h:h + 1, :]
        m = m_ref[h:h + 1, 0:1]
        for c in range(rows // L):
            sl = slice(c * L, (c + 1) * L)
            qc, kc, vc = q[sl], k[sl], v[sl]
            bc, lic = b_col[sl], li_col[sl]
            br, lir = b_row[:, sl], li_row[:, sl]
            qcb, kcb, vcb = qc.astype(BF16), kc.astype(BF16), vc.astype(BF16)
            dmat = jnp.where(causal, (bc - br) + lir, -jnp.inf)
            inter = bc + m
            m_t = jnp.maximum(inter, jnp.max(dmat, axis=-1, keepdims=True))
            s_inter = jnp.exp(inter - m_t)
            s = _dot_nt(qcb, kcb) * jnp.exp(dmat - m_t)
            num = _dot(s.astype(BF16), vcb) + s_inter * _dot(qcb, cmat.astype(BF16))
            den = (jnp.sum(s, axis=-1, keepdims=True)
                   + s_inter * jnp.sum(qc * nvec, axis=-1, keepdims=True))
            hc = num / jnp.maximum(jnp.abs(den), jnp.exp(-m_t))
            hn = _rms(hc, gmo_ref[h:h + 1, :]) * _sigmoid(o_ref[sl, hs])
            hm_ref[sl, hs] = hn.astype(BF16)
            b_last = bc[L - 1:L, :]
            g_row = (b_last - br) + lir
            g_col = (b_last - bc) + lic
            m_new = jnp.maximum(b_last + m, jnp.max(g_row, axis=-1, keepdims=True))
            s_old = jnp.exp(b_last + m - m_new)
            kw = kc * jnp.exp(g_col - m_new)
            cmat = s_old * cmat + _dot(kw.T.astype(BF16), vcb)
            nvec = s_old * nvec + jnp.sum(kw, axis=0, keepdims=True)
            m = m_new
        c_ref[h] = cmat
        n_ref[h:h + 1, :] = nvec
        m_ref[h:h + 1, :] = jnp.broadcast_to(m, (1, LANES))

    @pl.when(i == nsteps - 1)
    def _():
        conv_ref[...] = ext_ref[8 + rows - (CONV_W - 1):8 + rows, :]
        cst_ref[...] = c_ref[...]
        nst_ref[...] = n_ref[...]
        mst_ref[...] = m_ref[...]


def _mlstm_prompt(p, ps, bif_row, w_conv, b_conv, wqk, gmo, *, rows):
    t = p.shape[0]
    mw = GROUP
    hv = mw // M_HEADS
    hk = hv // 2
    nsteps = t // rows
    const = lambda shape: pl.BlockSpec(shape, lambda i: (0,) * len(shape))
    return pl.pallas_call(
        functools.partial(_mlstm_prompt_kernel, rows=rows, nsteps=nsteps),
        out_shape=(jax.ShapeDtypeStruct((t, mw), BF16),
                   jax.ShapeDtypeStruct((CONV_W - 1, mw), F32),
                   jax.ShapeDtypeStruct((M_HEADS, hk, hv), F32),
                   jax.ShapeDtypeStruct((8, hk), F32),
                   jax.ShapeDtypeStruct((8, LANES), F32)),
        grid=(nsteps,),
        in_specs=[
            pl.BlockSpec((rows, mw), lambda i: (i, C_UM)),
            pl.BlockSpec((rows, mw), lambda i: (i, C_VM)),
            pl.BlockSpec((rows, mw), lambda i: (i, C_OM)),
            pl.BlockSpec((rows, LANES), lambda i: (i, 0)),
            const((1, LANES)), const((CONV_W, mw)), const((1, mw)),
            const((M_HEADS, hv, 2 * hk)), const((M_HEADS, hv)),
        ],
        out_specs=(pl.BlockSpec((rows, mw), lambda i: (i, 0)),
                   const((CONV_W - 1, mw)), const((M_HEADS, hk, hv)), const((8, hk)), const((8, LANES))),
        scratch_shapes=[pltpu.VMEM((rows + 8, mw), F32), pltpu.VMEM((M_HEADS, hk, hv), F32),
                        pltpu.VMEM((8, hk), F32), pltpu.VMEM((8, LANES), F32)],
        compiler_params=_cparams(("arbitrary",)),
        name="mlstm_prompt",
    )(p, p, p, ps, bif_row, w_conv, b_conv, wqk, gmo)


def _mlstm_sample_kernel(u_ref, v_ref, o_ref, ps_ref, sc_ref, c0_ref, n0_ref, m0_ref,
                         bif_ref, wconv_ref, bconv_ref, wqk_ref, gmo_ref,
                         hm_ref, conv_ref, c1_ref, n1_ref, m1_ref, *, gb):
    mw = u_ref.shape[1]
    hv = mw // M_HEADS
    hk = hv // 2
    u = u_ref[...]
    wc = wconv_ref[...]
    cv = (sc_ref[0] * wc[0:1] + sc_ref[1] * wc[1:2] + sc_ref[2] * wc[2:3] + u * wc[3:4]) + bconv_ref[...]
    conv_ref[0] = sc_ref[1]
    conv_ref[1] = sc_ref[2]
    conv_ref[2] = u
    cb = (cv * _sigmoid(cv)).astype(BF16)
    g = ps_ref[...] + bif_ref[...]
    lf = _log_sigmoid(g)
    for h in range(M_HEADS):
        hs = slice(h * hv, (h + 1) * hv)
        qk = _dot(cb[:, hs], wqk_ref[h])
        q = qk[:, :hk] * (hk ** -0.5)
        k = qk[:, hk:]
        fill = jnp.zeros((LANES - gb, hk), F32)
        q_t = jnp.concatenate([q, fill], axis=0).T
        k_t = jnp.concatenate([k, fill], axis=0).T
        v = v_ref[:, hs]
        for s in range(gb):
            li = g[s:s + 1, S_IGATE + h:S_IGATE + h + 1]
            b = lf[s:s + 1, S_FGATE + h:S_FGATE + h + 1]
            m0 = m0_ref[s:s + 1, h:h + 1]
            qr, kr, vr = q[s:s + 1], k[s:s + 1], v[s:s + 1]
            cmat = c0_ref[s, h]
            nrow = n0_ref[s, h:h + 1, :]
            inter = b + m0
            m_t = jnp.maximum(inter, li)
            s_inter = jnp.exp(inter - m_t)
            dw = jnp.exp(li - m_t)
            sqk = jnp.sum(qr * kr, axis=-1, keepdims=True) * dw
            qc = jnp.sum(q_t[:, s:s + 1] * cmat, axis=0, keepdims=True)
            num = sqk * vr + s_inter * qc
            den = sqk + s_inter * jnp.sum(qr * nrow, axis=-1, keepdims=True)
            hc = num / jnp.maximum(jnp.abs(den), jnp.exp(-m_t))
            hn = _rms(hc, gmo_ref[h:h + 1, :]) * _sigmoid(o_ref[s:s + 1, hs])
            hm_ref[s:s + 1, hs] = hn.astype(BF16)
            c1_ref[s, h] = s_inter * cmat + (dw * k_t[:, s:s + 1]) * vr
            n1_ref[s, h:h + 1, :] = s_inter * nrow + dw * kr
            m1_ref[s:s + 1, h:h + 1] = m_t


def _mlstm_sample(p, ps, conv_t, c0, n0, m0, bif_row, w_conv, b_conv, wqk, gmo, *, gb):
    db = p.shape[0]
    mw = GROUP
    hv = mw // M_HEADS
    hk = hv // 2
    const = lambda shape: pl.BlockSpec(shape, lambda i: (0,) * len(shape))
    return pl.pallas_call(
        functools.partial(_mlstm_sample_kernel, gb=gb),
        out_shape=(jax.ShapeDtypeStruct((db, mw), BF16),
                   jax.ShapeDtypeStruct((CONV_W - 1, db, mw), F32),
                   jax.ShapeDtypeStruct((db, M_HEADS, hk, hv), F32),
                   jax.ShapeDtypeStruct((db, M_HEADS, hk), F32),
                   jax.ShapeDtypeStruct((db, M_HEADS), F32)),
        grid=(db // gb,),
        in_specs=[
            pl.BlockSpec((gb, mw), lambda i: (i, C_UM)),
            pl.BlockSpec((gb, mw), lambda i: (i, C_VM)),
            pl.BlockSpec((gb, mw), lambda i: (i, C_OM)),
            pl.BlockSpec((gb, LANES), lambda i: (i, 0)),
            pl.BlockSpec((CONV_W - 1, gb, mw), lambda i: (0, i, 0)),
            pl.BlockSpec((gb, M_HEADS, hk, hv), lambda i: (i, 0, 0, 0)),
            pl.BlockSpec((gb, M_HEADS, hk), lambda i: (i, 0, 0)),
            pl.BlockSpec((gb, M_HEADS), lambda i: (i, 0)),
            const((1, LANES)), const((CONV_W, mw)), const((1, mw)),
            const((M_HEADS, hv, 2 * hk)), const((M_HEADS, hv)),
        ],
        out_specs=(pl.BlockSpec((gb, mw), lambda i: (i, 0)),
                   pl.BlockSpec((CONV_W - 1, gb, mw), lambda i: (0, i, 0)),
                   pl.BlockSpec((gb, M_HEADS, hk, hv), lambda i: (i, 0, 0, 0)),
                   pl.BlockSpec((gb, M_HEADS, hk), lambda i: (i, 0, 0)),
                   pl.BlockSpec((gb, M_HEADS), lambda i: (i, 0))),
        compiler_params=_cparams(("arbitrary",)),
        name="mlstm_sample",
    )(p, p, p, ps, conv_t, c0, n0, m0, bif_row, w_conv, b_conv, wqk, gmo)


def _dsa_prep_kernel(k_ref, v_ref, ps_ref, gk_ref, k32_ref, v32_ref, kb_ref, vb_ref, ki_ref, kmx_ref):
    hd = gk_ref.shape[1]

    @pl.when(pl.program_id(0) == 0)
    def _():
        kmx_ref[...] = jnp.zeros(kmx_ref.shape, F32)

    for h in range(A_HEADS):
        hs = slice(h * hd, (h + 1) * hd)
        kn = _rms(k_ref[:, hs], gk_ref[...])
        k32_ref[:, hs] = kn
        kb = kn.astype(BF16)
        kb_ref[:, hs] = kb
        n2 = jnp.max(jnp.sum(jnp.square(kb.astype(F32)), axis=-1, keepdims=True), axis=0, keepdims=True)
        kmx_ref[h:h + 1, :] = jnp.maximum(kmx_ref[h:h + 1, :], n2)
    v = v_ref[...]
    v32_ref[...] = v
    vb_ref[...] = v.astype(BF16)
    ki_ref[...] = ps_ref[...].astype(BF16)


def _dsa_prep(p, ps, gk, *, tm):
    t = p.shape[0]
    hd = gk.shape[1]
    row = lambda w, c: pl.BlockSpec((tm, w), lambda i: (i, c))
    return pl.pallas_call(
        _dsa_prep_kernel,
        out_shape=(jax.ShapeDtypeStruct((t, GROUP), F32), jax.ShapeDtypeStruct((t, GROUP), F32),
                   jax.ShapeDtypeStruct((t, GROUP), BF16), jax.ShapeDtypeStruct((t, GROUP), BF16),
                   jax.ShapeDtypeStruct((t, LANES), BF16), jax.ShapeDtypeStruct((A_HEADS, LANES), F32)),
        grid=(t // tm,),
        in_specs=[row(GROUP, C_KA), row(GROUP, C_VA), row(LANES, 0),
                  pl.BlockSpec((1, hd), lambda i: (0, 0))],
        out_specs=(row(GROUP, 0), row(GROUP, 0), row(GROUP, 0), row(GROUP, 0), row(LANES, 0),
                   pl.BlockSpec((A_HEADS, LANES), lambda i: (0, 0))),
        compiler_params=_cparams(("arbitrary",)),
        name="dsa_prep",
    )(p, p, ps, gk)


def _dsa_prompt_kernel(q_ref, qi_ref, ps_ref, gq_ref, kmx_ref, ki_ref, kb_ref, vb_ref,
                       ha_ref, key_ref, qit_ref, qn_ref, wt_ref, mx_ref, l_ref, acc_ref, *, tq, topk):
    i = pl.program_id(0)
    hd = gq_ref.shape[1]
    ck = tq
    nk = i + 1
    kf = float(topk)
    SUB = 8

    def fold(x, op):
        out = x[0:SUB]
        for r in range(1, ck // SUB):
            out = op(out, x[r * SUB:(r + 1) * SUB])
        return out

    qit_ref[...] = qi_ref[...].T.astype(BF16)
    wt_ref[...] = ps_ref[...].T * ((IDX_HEADS ** -0.5) * (IDX_DIM ** -0.5))
    qscale = (hd ** -0.5) * LOG2E
    for h in range(A_HEADS):
        hs = slice(h * hd, (h + 1) * hd)
        qn_ref[:, hs] = (_rms(q_ref[:, hs], gq_ref[...]) * qscale).astype(BF16)

    q_pos = i * tq + lax.broadcasted_iota(I32, (ck, tq), 1)
    k_in = lax.broadcasted_iota(I32, (ck, tq), 0)

    def score_body(c, carry):
        off = pl.multiple_of(c * ck, ck)
        ki = ki_ref[pl.ds(off, ck), 0:IDX_DIM]
        acc = jnp.zeros((ck, tq), F32)
        for h in range(IDX_HEADS):
            s = _dot(ki, qit_ref[h * IDX_DIM:(h + 1) * IDX_DIM, :])
            acc = acc + jnp.maximum(s, 0.0) * wt_ref[S_WIDX + h:S_WIDX + h + 1, :]
        key_ref[pl.ds(off, ck), :] = jnp.where(k_in + c * ck <= q_pos, _mono_key(acc), INT_MIN)
        return carry

    lax.fori_loop(0, nk, score_body, 0)

    def count_where(pred):
        def one(c, acc):
            off = pl.multiple_of(c * ck, ck)
            return acc + fold(jnp.where(pred(key_ref[pl.ds(off, ck), :], c * ck), 1.0, 0.0), jnp.add)
        acc = lax.fori_loop(0, nk // 2, lambda c2, a: one(2 * c2 + 1, one(2 * c2, a)), jnp.zeros((SUB, tq), F32))
        acc = lax.cond(nk % 2 == 1, lambda a: one(nk - 1, a), lambda a: a, acc)
        return jnp.sum(acc, axis=0, keepdims=True)

    def count_ge(cand):
        return count_where(lambda kc, base: kc >= cand)

    cnt0 = count_ge(jnp.zeros((1, tq), I32))
    ok0 = cnt0 >= kf
    thr0 = jnp.where(ok0, 0, INT_MIN).astype(I32)
    cnt_t0 = jnp.where(ok0, cnt0, kf)

    def bit_body(b, carry):
        thr, cnt_t = carry
        cand = thr + jnp.left_shift(jnp.int32(1), 30 - b)
        cnt = count_ge(cand)
        ok = cnt >= kf
        return jnp.where(ok, cand, thr), jnp.where(ok, cnt, cnt_t)

    thr, cnt_t = lax.fori_loop(0, 31, bit_body, (thr0, cnt_t0))
    has_tie = jnp.max(cnt_t) > kf
    bias_on, bias_off = 0.0, -jnp.inf

    @pl.when(jnp.logical_not(has_tie))
    def _():
        tmin = jnp.maximum(thr, INT_MIN + 1)

        def body(c, carry):
            off = pl.multiple_of(c * ck, ck)
            bias = jnp.where(key_ref[pl.ds(off, ck), :] >= tmin, bias_on, bias_off)
            key_ref[pl.ds(off, ck), :] = lax.bitcast_convert_type(bias.T, I32)
            return carry
        lax.fori_loop(0, nk, body, 0)

    @pl.when(has_tie)
    def _():
        n_gt = count_where(lambda kc, base: kc > thr)
        need = kf - n_gt

        def pos_body(b, x):
            cand = x + jnp.left_shift(jnp.int32(1), b)
            cnt = count_where(lambda kc, base: (kc == thr) & (k_in + base < cand))
            return jnp.where(cnt < need, cand, x)

        nbits = int(math.ceil(math.log2(key_ref.shape[0]))) + 1
        x = lax.fori_loop(0, nbits, lambda b, x: pos_body(nbits - 1 - b, x), jnp.zeros((1, tq), I32))
        x = jnp.where(thr > INT_MIN, x, -1)

        def body(c, carry):
            off = pl.multiple_of(c * ck, ck)
            kc = key_ref[pl.ds(off, ck), :]
            sel = (kc > thr) | ((kc == thr) & (k_in + c * ck <= x))
            key_ref[pl.ds(off, ck), :] = lax.bitcast_convert_type(jnp.where(sel, bias_on, bias_off).T, I32)
            return carry
        lax.fori_loop(0, nk, body, 0)

    ncol = ck // LANES

    def lane_fold(x, op):
        out = x[:, 0:LANES]
        for r in range(1, ncol):
            out = op(out, x[:, r * LANES:(r + 1) * LANES])
        return out

    def scores(c, h):
        off = pl.multiple_of(c * ck, ck)
        hs = slice(h * hd, (h + 1) * hd)
        bias = lax.bitcast_convert_type(key_ref[pl.ds(off, ck), :], F32)
        return _dot_nt(qn_ref[:, hs], kb_ref[pl.ds(off, ck), hs]) + bias

    def sum_pass():
        acc_ref[...] = jnp.zeros(acc_ref.shape, F32)
        l_ref[...] = jnp.zeros(l_ref.shape, F32)

        def sum_body(c, carry):
            off = pl.multiple_of(c * ck, ck)
            for h in range(A_HEADS):
                hs = slice(h * hd, (h + 1) * hd)
                p = jnp.exp2(scores(c, h) - jnp.tile(mx_ref[h], (1, ncol)))
                l_ref[h] += lane_fold(p, jnp.add)
                acc_ref[h] += _dot(p.astype(BF16), vb_ref[pl.ds(off, ck), hs])
            return carry

        lax.fori_loop(0, nk, sum_body, 0)
        return [jnp.sum(l_ref[h], axis=-1, keepdims=True) for h in range(A_HEADS)]

    for h in range(A_HEADS):
        qh = qn_ref[:, h * hd:(h + 1) * hd].astype(F32)
        bound = jnp.sqrt(jnp.sum(qh * qh, axis=-1, keepdims=True) * kmx_ref[h:h + 1, 0:1])
        mx_ref[h] = jnp.broadcast_to(bound, (tq, LANES))
    l_cols = sum_pass()
    l_min = l_cols[0]
    for h in range(1, A_HEADS):
        l_min = jnp.minimum(l_min, l_cols[h])

    def finish(l_cols):
        for h in range(A_HEADS):
            ha_ref[:, h * hd:(h + 1) * hd] = (acc_ref[h] / l_cols[h]).astype(BF16)

    finish(l_cols)

    @pl.when(jnp.logical_not(jnp.min(l_min) >= SOFTMAX_SUM_FLOOR))
    def _():
        mx_ref[...] = jnp.full(mx_ref.shape, -jnp.inf, F32)

        def max_body(c, carry):
            for h in range(A_HEADS):
                mx_ref[h] = jnp.maximum(mx_ref[h], lane_fold(scores(c, h), jnp.maximum))
            return carry

        lax.fori_loop(0, nk, max_body, 0)
        for h in range(A_HEADS):
            mx_ref[h] = jnp.broadcast_to(jnp.max(mx_ref[h], axis=-1, keepdims=True), (tq, LANES))
        finish(sum_pass())


def _dsa_prompt(p, ps, gq, kmx, ki, kb, vb, *, tq):
    t = p.shape[0]
    hd = gq.shape[1]
    topk = min(TOPK_MAX, t // 4)
    return pl.pallas_call(
        functools.partial(_dsa_prompt_kernel, tq=tq, topk=topk),
        out_shape=jax.ShapeDtypeStruct((t, GROUP), BF16),
        grid=(t // tq,),
        in_specs=[
            pl.BlockSpec((tq, GROUP), lambda i: (i, C_QA)),
            pl.BlockSpec((tq, GROUP), lambda i: (i, C_QI)),
            pl.BlockSpec((tq, LANES), lambda i: (i, 0)),
            pl.BlockSpec((1, hd), lambda i: (0, 0)),
            pl.BlockSpec((A_HEADS, LANES), lambda i: (0, 0)),
            _vmem_spec(), _vmem_spec(), _vmem_spec(),
        ],
        out_specs=pl.BlockSpec((tq, GROUP), lambda i: (i, 0)),
        scratch_shapes=[pltpu.VMEM((t, tq), I32),
                        pltpu.VMEM((IDX_HEADS * IDX_DIM, tq), BF16),
                        pltpu.VMEM((tq, GROUP), BF16),
                        pltpu.VMEM((LANES, tq), F32),
                        pltpu.VMEM((A_HEADS, tq, LANES), F32),
                        pltpu.VMEM((A_HEADS, tq, LANES), F32),
                        pltpu.VMEM((A_HEADS, tq, hd), F32)],
        compiler_params=_cparams(("arbitrary",)),
        name="dsa_prompt",
    )(p, p, ps, gq, kmx, ki, kb, vb)


def _dsa_sample_prep_kernel(q_ref, k_ref, v_ref, qi_ref, ps_ref, gq_ref, gk_ref,
                            qs_ref, kn_ref, vn_ref, sn_ref):
    hd = gq_ref.shape[1]
    qscale = (hd ** -0.5) * LOG2E
    for h in range(A_HEADS):
        hs = slice(h * hd, (h + 1) * hd)
        qs_ref[:, hs] = _rms(q_ref[:, hs], gq_ref[...]) * qscale
        kn_ref[:, hs] = _rms(k_ref[:, hs], gk_ref[...])
    vn_ref[...] = v_ref[...]
    ps = ps_ref[...]
    lane = lax.broadcasted_iota(I32, ps.shape, 1)
    ke = jnp.where(lane < IDX_DIM, ps, 0.0)
    ko = pltpu.roll(ke, IDX_DIM, axis=1)
    w_scale = (IDX_HEADS ** -0.5) * (IDX_DIM ** -0.5)
    acc = jnp.zeros((ps.shape[0], 1), F32)
    for pr in range(IDX_HEADS // 2):
        q2 = qi_ref[:, pr * LANES:(pr + 1) * LANES]
        se = jnp.sum(q2 * ke, axis=-1, keepdims=True)
        so = jnp.sum(q2 * ko, axis=-1, keepdims=True)
        acc = acc + jnp.maximum(se, 0.0) * (ps[:, S_WIDX + 2 * pr:S_WIDX + 2 * pr + 1] * w_scale)
        acc = acc + jnp.maximum(so, 0.0) * (ps[:, S_WIDX + 2 * pr + 1:S_WIDX + 2 * pr + 2] * w_scale)
    sn_ref[...] = jnp.broadcast_to(acc, sn_ref.shape)


def _dsa_sample_prep(p, ps, gq, gk):
    db = p.shape[0]
    hd = gq.shape[1]
    blk = lambda c: pl.BlockSpec((db, GROUP), lambda i: (0, c))
    full = lambda shape: pl.BlockSpec(shape, lambda i: (0,) * len(shape))
    return pl.pallas_call(
        _dsa_sample_prep_kernel,
        out_shape=(jax.ShapeDtypeStruct((db, GROUP), F32), jax.ShapeDtypeStruct((db, GROUP), F32),
                   jax.ShapeDtypeStruct((db, GROUP), F32), jax.ShapeDtypeStruct((db, LANES), F32)),
        grid=(1,),
        in_specs=[blk(C_QA), blk(C_KA), blk(C_VA), blk(C_QI), full((db, LANES)), full((1, hd)), full((1, hd))],
        out_specs=(full((db, GROUP)), full((db, GROUP)), full((db, GROUP)), full((db, LANES))),
        compiler_params=_cparams(("arbitrary",)),
        name="dsa_sample_prep",
    )(p, p, p, p, ps, gq, gk)


def _dsa_sample_scores_kernel(pt_ref, qi_ref, w_ref, ckt_ref, out_ref, pbuf, sem, *, n_pages, nseq):
    b = pl.program_id(0)
    slot = b % 2

    def page_copy(seq, p, sl):
        return pltpu.make_async_copy(ckt_ref.at[0, pt_ref[seq, p]], pbuf.at[sl, p], sem.at[sl])

    def issue(seq, sl):
        def body(p, carry):
            page_copy(seq, p, sl).start()
            return carry
        lax.fori_loop(0, n_pages, body, 0, unroll=8)

    @pl.when(b == 0)
    def _():
        issue(0, 0)

    @pl.when(b + 1 < nseq)
    def _():
        issue(b + 1, 1 - slot)

    def wait_body(p, carry):
        page_copy(b, p, slot).wait()
        return carry
    lax.fori_loop(0, n_pages, wait_body, 0, unroll=8)

    qi = qi_ref[0].astype(BF16)
    w = w_ref[0]
    two = 2 * PAGE_SIZE

    def score_body(p2, carry):
        kt = jnp.concatenate([pbuf[slot, 2 * p2], pbuf[slot, 2 * p2 + 1]], axis=1).astype(BF16)
        s = jnp.maximum(_dot(qi, kt), 0.0) * w
        out_ref[0, :, pl.ds(pl.multiple_of(p2 * two, two), two)] = jnp.sum(s, axis=0, keepdims=True)
        return carry
    lax.fori_loop(0, n_pages // 2, score_body, 0, unroll=8)


def _dsa_sample_scores(page_table, qi3, w3, cache_kidx_t):
    db, n_pages = page_table.shape
    assert n_pages % 16 == 0
    grid_spec = pltpu.PrefetchScalarGridSpec(
        num_scalar_prefetch=1,
        grid=(db,),
        in_specs=[pl.BlockSpec((1, IDX_HEADS, IDX_DIM), lambda b, pt: (b, 0, 0)),
                  pl.BlockSpec((1, IDX_HEADS, 1), lambda b, pt: (b, 0, 0)),
                  pl.BlockSpec(memory_space=pl.ANY)],
        out_specs=pl.BlockSpec((1, 1, n_pages * PAGE_SIZE), lambda b, pt: (b, 0, 0)),
        scratch_shapes=[pltpu.VMEM((2, n_pages, IDX_DIM, PAGE_SIZE), F32), pltpu.SemaphoreType.DMA((2,))],
    )
    return pl.pallas_call(
        functools.partial(_dsa_sample_scores_kernel, n_pages=n_pages, nseq=db),
        out_shape=jax.ShapeDtypeStruct((db, 1, n_pages * PAGE_SIZE), F32),
        grid_spec=grid_spec,
        compiler_params=_cparams(("arbitrary",)),
        name="dsa_sample_scores",
    )(page_table, qi3, w3, cache_kidx_t)


def _dsa_sample_select_kernel(sc_ref, sn_ref, ptt_ref, idx_ref, nv_ref, ns_ref, *, topk):
    db, npg, pgs = sc_ref.shape
    kf = float(topk)
    key = _mono_key(sc_ref[...])
    key_n = _mono_key(sn_ref[...][:, 0:1]).reshape(db, 1, 1)

    def total(x):
        return jnp.sum(jnp.sum(x, axis=2, keepdims=True), axis=1, keepdims=True)

    def count_ge(cand):
        return (total(jnp.where(key >= cand, 1.0, 0.0)) + jnp.where(key_n >= cand, 1.0, 0.0))

    cnt0 = count_ge(jnp.zeros((db, 1, 1), I32))
    thr0 = jnp.where(cnt0 >= kf, 0, INT_MIN).astype(I32)

    def bit_body(b, thr):
        cand = thr + jnp.left_shift(jnp.int32(1), 30 - b)
        return jnp.where(count_ge(cand) >= kf, cand, thr)

    thr = lax.fori_loop(0, 31, bit_body, thr0)
    gt = jnp.where(key > thr, 1.0, 0.0)
    eq = jnp.where(key == thr, 1.0, 0.0)
    gt_n = jnp.where(key_n > thr, 1.0, 0.0)
    eq_n = jnp.where(key_n == thr, 1.0, 0.0)
    need = kf - (total(gt) + gt_n)

    r_i = lax.broadcasted_iota(I32, (pgs, pgs), 0)
    c_i = lax.broadcasted_iota(I32, (pgs, pgs), 1)
    upper = jnp.where(r_i <= c_i, 1.0, 0.0).astype(BF16)
    lower = jnp.where(c_i <= r_i, 1.0, 0.0).astype(BF16)
    lower_pg = jnp.where(lax.broadcasted_iota(I32, (npg, npg), 1)
                         <= lax.broadcasted_iota(I32, (npg, npg), 0), 1.0, 0.0).astype(BF16)
    j_row = lax.broadcasted_iota(I32, (1, topk), 1).astype(F32)
    p_col = lax.broadcasted_iota(I32, (npg, 1), 0).astype(F32)

    def page_prefix(x):
        pin = _dot(x.astype(BF16), upper)
        rt = pin[:, pgs - 1:pgs]
        incl = _dot(lower_pg, jnp.broadcast_to(rt, (npg, pgs)).astype(BF16))[:, 0:1]
        return pin, rt, incl

    for b in range(db):
        eq_b = eq[b]
        pin_e, rt_e, incl_e = page_prefix(eq_b)
        rank_e = (incl_e - rt_e) + pin_e
        need_b = need[b]
        sel = gt[b] + jnp.where(rank_e <= need_b, eq_b, 0.0)
        n_eq = incl_e[npg - 1:npg, :]
        new_sel = gt_n[b] + jnp.where(n_eq + 1.0 <= need_b, eq_n[b], 0.0)

        pin, rt, incl = page_prefix(sel)
        ex = incl - rt
        n_sel = incl[npg - 1:npg, :]
        page_j = jnp.sum(jnp.where(incl <= j_row, 1.0, 0.0), axis=0, keepdims=True)
        oh_t = p_col == page_j
        ex_j = jnp.sum(jnp.where(oh_t, ex, 0.0), axis=0, keepdims=True)
        r_j = j_row - ex_j + 1.0
        pin_t = _dot(lower, sel.T.astype(BF16))
        rows_t = _dot(pin_t.astype(BF16), jnp.where(oh_t, 1.0, 0.0).astype(BF16))
        off_j = jnp.sum(jnp.where(rows_t < r_j, 1.0, 0.0), axis=0, keepdims=True)
        phys_j = jnp.sum(jnp.where(oh_t, ptt_ref[:, b:b + 1], 0.0), axis=0, keepdims=True)
        idx = jnp.where(j_row < n_sel, phys_j * float(pgs) + off_j, 0.0)
        idx_ref[b:b + 1, :] = idx.astype(I32)
        nv_ref[b:b + 1, :] = jnp.broadcast_to(n_sel, (1, LANES)).astype(I32)
        ns_ref[b:b + 1, :] = jnp.broadcast_to(new_sel, (1, LANES)).astype(I32)


def _dsa_sample_select(sc3, sn, pt_t, *, topk):
    db = sc3.shape[0]
    return pl.pallas_call(
        functools.partial(_dsa_sample_select_kernel, topk=topk),
        out_shape=(jax.ShapeDtypeStruct((db, topk), I32), jax.ShapeDtypeStruct((db, LANES), I32),
                   jax.ShapeDtypeStruct((db, LANES), I32)),
        in_specs=[_vmem_spec(), _vmem_spec(), _vmem_spec()],
        out_specs=(_vmem_spec(), _vmem_spec(), _vmem_spec()),
        compiler_params=_cparams(None),
        name="dsa_sample_select",
    )(sc3, sn, pt_t)


def _dsa_sample_attend_kernel(idx_ref, nv_ref, ns_ref, q_ref, kn_ref, vn_ref, ck_ref, cv_ref,
                              o_ref, kbuf, vbuf, sem, *, topk, nseq):
    b = pl.program_id(0)
    slot = b % 2
    nh = A_HEADS
    hd = q_ref.shape[2] // nh

    def row_copies(seq, j, sl):
        r = idx_ref[seq, j]
        dst = pl.ds(pl.multiple_of(j * nh, nh), nh)
        return (pltpu.make_async_copy(ck_ref.at[0, r], kbuf.at[sl, dst], sem.at[0, sl]),
                pltpu.make_async_copy(cv_ref.at[0, r], vbuf.at[sl, dst], sem.at[1, sl]))

    def issue(seq, sl):
        def body(j, carry):
            ck, cv = row_copies(seq, j, sl)
            ck.start()
            cv.start()
            return carry
        lax.fori_loop(0, topk, body, 0, unroll=8)

    @pl.when(b == 0)
    def _():
        issue(0, 0)

    @pl.when(b + 1 < nseq)
    def _():
        issue(b + 1, 1 - slot)

    def wait_body(j, carry):
        ck, cv = row_copies(b, j, slot)
        ck.wait()
        cv.wait()
        return carry
    lax.fori_loop(0, topk, wait_body, 0, unroll=8)

    n_valid = nv_ref[b]
    new_on = ns_ref[b] > 0
    j_row = lax.broadcasted_iota(I32, (1, topk), 1)
    for h in range(nh):
        hs = slice(h * hd, (h + 1) * hd)
        qh = q_ref[0, :, hs]
        kh = kbuf[slot, pl.ds(h, topk, stride=nh), :]
        vh = vbuf[slot, pl.ds(h, topk, stride=nh), :]
        s = _dot_nt(qh.astype(BF16), kh.astype(BF16))
        s = jnp.where(j_row < n_valid, s, -jnp.inf)
        s_new = jnp.sum(qh * kn_ref[0, :, hs], axis=-1, keepdims=True)
        s_new = jnp.where(new_on, s_new, -jnp.inf)
        m = jnp.maximum(jnp.max(s, axis=-1, keepdims=True), s_new)
        p = jnp.exp2(s - m)
        p_new = jnp.exp2(s_new - m)
        l = jnp.sum(p, axis=-1, keepdims=True) + p_new
        o = _dot(p.astype(BF16), vh.astype(BF16)) + p_new * vn_ref[0, :, hs]
        o_ref[0, :, hs] = o / l


def _dsa_sample_attend(idx, nv, ns, qs3, kn3, vn3, cache_k, cache_v, *, topk):
    db = idx.shape[0]
    w = qs3.shape[2]
    hd = w // A_HEADS
    row = pl.BlockSpec((1, 1, w), lambda b, *_: (b, 0, 0))
    grid_spec = pltpu.PrefetchScalarGridSpec(
        num_scalar_prefetch=3,
        grid=(db,),
        in_specs=[row, row, row, pl.BlockSpec(memory_space=pl.ANY), pl.BlockSpec(memory_space=pl.ANY)],
        out_specs=row,
        scratch_shapes=[pltpu.VMEM((2, topk * A_HEADS, hd), F32), pltpu.VMEM((2, topk * A_HEADS, hd), F32),
                        pltpu.SemaphoreType.DMA((2, 2))],
    )
    return pl.pallas_call(
        functools.partial(_dsa_sample_attend_kernel, topk=topk, nseq=db),
        out_shape=jax.ShapeDtypeStruct((db, 1, w), F32),
        grid_spec=grid_spec,
        compiler_params=_cparams(("arbitrary",)),
        name="dsa_sample_attend",
    )(idx, nv, ns, qs3, kn3, vn3, cache_k, cache_v)


def _merge_kernel(hm_ref, ha_ref, gm_ref, ga_ref, x_ref, wmo_ref, wao_ref, wout_ref, y_ref):
    t1 = _dot(hm_ref[...].astype(BF16), wmo_ref[...])
    t2 = _dot(ha_ref[...].astype(BF16), wao_ref[...])
    z = _sigmoid(gm_ref[...]) * t1 + _sigmoid(ga_ref[...]) * t2
    y_ref[...] = x_ref[...] + _dot(z.astype(BF16), wout_ref[...])


def _merge(hm, ha, p, x, wmo, wao, wout, *, tm):
    m, d = x.shape
    w = hm.shape[1]
    return pl.pallas_call(
        _merge_kernel,
        out_shape=jax.ShapeDtypeStruct((m, d), F32),
        grid=(pl.cdiv(m, tm),),
        in_specs=[
            pl.BlockSpec((tm, w), lambda i: (i, 0)),
            pl.BlockSpec((tm, w), lambda i: (i, 0)),
            pl.BlockSpec((tm, d), lambda i: (i, 0)),
            pl.BlockSpec((tm, d), lambda i: (i, 1)),
            pl.BlockSpec((tm, d), lambda i: (i, 0)),
            _vmem_spec(), _vmem_spec(), _vmem_spec(),
        ],
        out_specs=pl.BlockSpec((tm, d), lambda i: (i, 0)),
        compiler_params=_cparams(("arbitrary",)),
        name="merge",
    )(hm, ha, p, p, x, wmo, wao, wout)


FFN_TM = 512
FFN_TF = 512
PROJ_TM = 512
MERGE_TM = 256
MLSTM_ROWS = 256
MLSTM_SAMPLE_GROUP = 8
DSA_TQ = 256
DSA_PREP_TM = 512


def kernel(x_prompt, x_sample, cache_k, cache_v, cache_kidx, state_conv, state_C, state_n, state_m, page_table, norm_ffn1, w_ffn1_up, w_ffn1_down, norm_mix, w_in, b_if, w_conv, b_conv, w_qk_m, norm_mlstm_out, norm_q, norm_k, w_mo, w_ao, w_out, norm_ffn2, w_ffn2_up, w_ffn2_down):
    depth = w_in.shape[0]
    assert depth == 1, "single-layer step"
    bsz, seq, d = x_prompt.shape
    assert bsz == 1
    db, dseq, _ = x_sample.shape
    assert dseq == 1
    n_pages = page_table.shape[1]
    assert cache_k.shape[2] == PAGE_SIZE
    past = n_pages * PAGE_SIZE
    topk_s = min(TOPK_MAX, (past + dseq) // 4)
    l = 0

    xp = x_prompt.reshape(seq, d)
    xs = x_sample.reshape(db, d)

    wu1, wd1 = w_ffn1_up[l].astype(BF16), w_ffn1_down[l].astype(BF16)
    wu2, wd2 = w_ffn2_up[l].astype(BF16), w_ffn2_down[l].astype(BF16)
    w_pad, w_small, tab = _prep_w_in(w_in[l], d)
    wmo, wao, wout = w_mo[l].astype(BF16), w_ao[l].astype(BF16), w_out[l].astype(BF16)
    wqk = w_qk_m[l].astype(BF16)
    g1, gmix, g2 = norm_ffn1[l][None], norm_mix[l][None], norm_ffn2[l][None]
    gq, gk = norm_q[l][None], norm_k[l][None]
    gmo = norm_mlstm_out[l]
    bif_row = jnp.zeros((1, LANES), F32).at[0, S_IGATE:S_IGATE + 2 * M_HEADS].set(b_if[l])
    wcv, bcv = w_conv[l], b_conv[l][None]

    xp1 = _ffn(xp, g1, wu1, wd1, tm=FFN_TM, tf=FFN_TF)
    xs1 = _ffn(xs, g1, wu1, wd1, tm=db, tf=FFN_TF)

    pp, pps = _inproj(xp1, gmix, w_pad, w_small, tab, tm=PROJ_TM)
    sp, sps = _inproj(xs1, gmix, w_pad, w_small, tab, tm=db)

    hm_p, conv_p, c_p, n_p, m_p = _mlstm_prompt(pp, pps, bif_row, wcv, bcv, wqk, gmo, rows=MLSTM_ROWS)
    k32, v32, kb, vb, ki, kmx = _dsa_prep(pp, pps, gk, tm=DSA_PREP_TM)
    ha_p = _dsa_prompt(pp, pps, gq, kmx, ki, kb, vb, tq=DSA_TQ)
    yp = _merge(hm_p, ha_p, pp, xp1, wmo, wao, wout, tm=MERGE_TM)
    out_p = _ffn(yp, g2, wu2, wd2, tm=FFN_TM, tf=FFN_TF)

    conv_t = jnp.transpose(state_conv[l], (1, 0, 2))
    hm_s, conv_s, c_s, n_s, m_s = _mlstm_sample(sp, sps, conv_t, state_C[l], state_n[l], state_m[l],
                                                bif_row, wcv, bcv, wqk, gmo, gb=MLSTM_SAMPLE_GROUP)
    qs, kn_s, vn_s, sn = _dsa_sample_prep(sp, sps, gq, gk)
    qi3 = sp[:, C_QI * GROUP:(C_QI + 1) * GROUP].reshape(db, IDX_HEADS, IDX_DIM)
    w3 = (sps[:, S_WIDX:S_WIDX + IDX_HEADS] * ((IDX_HEADS ** -0.5) * (IDX_DIM ** -0.5))).reshape(db, IDX_HEADS, 1)
    sc = _dsa_sample_scores(page_table, qi3, w3, jnp.swapaxes(cache_kidx, 2, 3))
    sc3 = sc.reshape(db, n_pages, PAGE_SIZE)
    idx, nv, ns = _dsa_sample_select(sc3, sn, page_table.T.astype(F32), topk=topk_s)
    n_rows = cache_k.shape[1] * PAGE_SIZE
    ha_s = _dsa_sample_attend(idx, nv[:, 0], ns[:, 0], qs.reshape(db, 1, GROUP),
                              kn_s.reshape(db, 1, GROUP), vn_s.reshape(db, 1, GROUP),
                              cache_k.reshape(depth, n_rows, A_HEADS, GROUP // A_HEADS),
                              cache_v.reshape(depth, n_rows, A_HEADS, GROUP // A_HEADS), topk=topk_s)
    ys = _merge(hm_s, ha_s.reshape(db, GROUP), sp, xs1, wmo, wao, wout, tm=db)
    out_s = _ffn(ys, g2, wu2, wd2, tm=db, tf=FFN_TF)

    hd = GROUP // A_HEADS
    hv = GROUP // M_HEADS
    hk = hv // 2
    return (
        out_p.reshape(bsz, seq, d),
        out_s.reshape(db, dseq, d),
        k32.reshape(depth, bsz, seq, A_HEADS, hd),
        v32.reshape(depth, bsz, seq, A_HEADS, hd),
        pps[:, S_KIDX:S_KIDX + IDX_DIM].reshape(depth, bsz, seq, IDX_DIM),
        conv_p.reshape(depth, bsz, CONV_W - 1, GROUP),
        c_p.reshape(depth, bsz, M_HEADS, hk, hv),
        n_p[:M_HEADS].reshape(depth, bsz, M_HEADS, hk),
        m_p[:M_HEADS, 0].reshape(depth, bsz, M_HEADS),
        kn_s.reshape(depth, db, dseq, A_HEADS, hd),
        vn_s.reshape(depth, db, dseq, A_HEADS, hd),
        sps[:, S_KIDX:S_KIDX + IDX_DIM].reshape(depth, db, dseq, IDX_DIM),
        jnp.transpose(conv_s, (1, 0, 2)).reshape(depth, db, CONV_W - 1, GROUP),
        c_s.reshape(depth, db, M_HEADS, hk, hv),
        n_s.reshape(depth, db, M_HEADS, hk),
        m_s.reshape(depth, db, M_HEADS),
    )
```

```python
import functools
import math

import jax
import jax.numpy as jnp
from jax import lax
from jax.experimental import pallas as pl
from jax.experimental.pallas import tpu as pltpu

F32 = jnp.float32
BF16 = jnp.bfloat16
I32 = jnp.int32

EPS = 1e-6
LANES = 128
PAGE_SIZE = 128
M_HEADS = 4
M_CHUNK = 64
CONV_W = 4
A_HEADS = 8
IDX_HEADS = 16
IDX_DIM = 64
TOPK_MAX = 256
N_BRANCHES = 2
GROUP = 1024
INT_MIN = -2 ** 31
NEG_INIT = -1e30
LOG2E = 1.4426950408889634
SOFTMAX_SUM_FLOOR = 2.0 ** -100
VMEM_LIMIT = 56 * 1024 * 1024

S_KIDX = 0
S_WIDX = IDX_DIM
S_IGATE = IDX_DIM + IDX_HEADS
S_FGATE = S_IGATE + M_HEADS

C_GATE, C_UM, C_VM, C_OM, C_QA, C_KA, C_VA, C_QI = 0, 4, 5, 6, 7, 8, 9, 10
N_BIG_BLOCKS = 11


def _cparams(sem, vmem=VMEM_LIMIT):
    return pltpu.CompilerParams(dimension_semantics=sem, vmem_limit_bytes=vmem)


def _vmem_spec():
    return pl.BlockSpec(memory_space=pltpu.VMEM)


def _rms(x, g):
    return x * lax.rsqrt(jnp.mean(x * x, axis=-1, keepdims=True) + EPS) * g


def _sigmoid(x):
    return 1.0 / (1.0 + jnp.exp(-x))


def _log_sigmoid(x):
    return jnp.minimum(x, 0.0) - jnp.log(1.0 + jnp.exp(-jnp.abs(x)))


def _dot(a, b):
    return jnp.dot(a, b, preferred_element_type=F32)


def _dot_nt(a, b):
    return lax.dot_general(a, b, (((1,), (1,)), ((), ())), preferred_element_type=F32)


def _dot_exact(a, b):
    return jnp.dot(a, b, preferred_element_type=F32, precision=lax.Precision.HIGHEST)


def _mono_key(x):
    b = lax.bitcast_convert_type(x, I32)
    return b ^ ((b >> 31) & jnp.int32(0x7FFFFFFF))


def _ffn_kernel(x_ref, g_ref, gn_ref, wa_ref, wb_ref, wd_ref, wat_ref, wbt_ref, wdt_ref, o_ref, *rest,
                nfull, emit_next):
    hn_ref, h_ref = rest if emit_next else (None, rest[0])
    j = pl.program_id(1)

    @pl.when(j == 0)
    def _():
        h_ref[...] = _rms(x_ref[...], g_ref[...]).astype(BF16)

    def half_step(wa, wb, wd):
        h = h_ref[...]
        a = _dot(h, wa[...])
        act = (a * _sigmoid(a) * _dot(h, wb[...])).astype(BF16)
        return _dot(act, wd[...])

    @pl.when(j == 0)
    def _():
        o_ref[...] = half_step(wa_ref, wb_ref, wd_ref)

    @pl.when((j > 0) & (j < nfull))
    def _():
        o_ref[...] += half_step(wa_ref, wb_ref, wd_ref)

    @pl.when(j == nfull)
    def _():
        y = x_ref[...] + 0.5 * (o_ref[...] + half_step(wat_ref, wbt_ref, wdt_ref))
        o_ref[...] = y
        if emit_next:
            hn_ref[...] = _rms(y, gn_ref[...]).astype(BF16)


def _ffn(x, g, w_up, w_down, *, tm, tf, g_next=None):
    emit_next = g_next is not None
    m, d = x.shape
    d_ff = w_down.shape[0]
    nfull = d_ff // tf
    tail = d_ff - nfull * tf
    assert tail > 0 and tail % LANES == 0 and d_ff % LANES == 0
    elem = lambda r, c: (pl.Element(r), pl.Element(c))
    main = lambda j: jnp.minimum(j, nfull - 1) * tf
    main_b = lambda j: (d_ff // LANES + jnp.minimum(j, nfull - 1) * (tf // LANES)) * LANES
    tail_b = (d_ff + nfull * tf) // LANES
    row_out = pl.BlockSpec((tm, d), lambda i, j: (i, 0))
    y_shape = jax.ShapeDtypeStruct((m, d), F32)
    return pl.pallas_call(
        functools.partial(_ffn_kernel, nfull=nfull, emit_next=emit_next),
        out_shape=(y_shape, jax.ShapeDtypeStruct((m, d), BF16)) if emit_next else y_shape,
        grid=(pl.cdiv(m, tm), nfull + 1),
        in_specs=[
            pl.BlockSpec((tm, d), lambda i, j: (i, 0)),
            pl.BlockSpec((1, d), lambda i, j: (0, 0)),
            pl.BlockSpec((1, d), lambda i, j: (0, 0)),
            pl.BlockSpec(elem(d, tf), lambda i, j: (0, main(j))),
            pl.BlockSpec(elem(d, tf), lambda i, j: (0, main_b(j))),
            pl.BlockSpec(elem(tf, d), lambda i, j: (main(j), 0)),
            pl.BlockSpec(elem(d, tail), lambda i, j: (0, nfull * tf)),
            pl.BlockSpec(elem(d, tail), lambda i, j: (0, tail_b * LANES)),
            pl.BlockSpec(elem(tail, d), lambda i, j: (nfull * tf, 0)),
        ],
        out_specs=(row_out, row_out) if emit_next else row_out,
        scratch_shapes=[pltpu.VMEM((tm, d), BF16)],
        compiler_params=_cparams(("arbitrary", "arbitrary")),
        name="ffn",
    )(x, g, g_next if emit_next else g, w_up, w_up, w_down, w_up, w_up, w_down)


SUBLANES = 8


def _inproj_kernel(tab_ref, h_ref, wt_ref, p_ref, w_ref):
    @pl.when(pl.program_id(1) == 0)
    def _():
        w_ref[...] = wt_ref[...].T.astype(BF16)

    p_ref[...] = _dot(h_ref[...], w_ref[...])


def _inproj(h, w_t, tab, *, tm):
    m, d = h.shape
    n_groups = tab.shape[0]
    grid_spec = pltpu.PrefetchScalarGridSpec(
        num_scalar_prefetch=1,
        grid=(n_groups, pl.cdiv(m, tm)),
        in_specs=[
            pl.BlockSpec((tm, d), lambda g, i, tab: (i, 0)),
            pl.BlockSpec((pl.Element(GROUP), pl.Element(d)), lambda g, i, tab: (tab[g] * SUBLANES, 0)),
        ],
        out_specs=pl.BlockSpec((tm, GROUP), lambda g, i, tab: (i, g)),
        scratch_shapes=[pltpu.VMEM((d, GROUP), BF16)],
    )
    return pl.pallas_call(
        _inproj_kernel,
        out_shape=jax.ShapeDtypeStruct((m, n_groups * GROUP), F32),
        grid_spec=grid_spec,
        compiler_params=_cparams(("arbitrary", "arbitrary")),
        name="inproj",
    )(tab, h, w_t)


def _inproj_small_kernel(h_ref, ws_ref, ps_ref):
    ps_ref[...] = _dot(h_ref[...], ws_ref[...])


def _inproj_small(h, w_small, *, tm):
    m, d = h.shape
    return pl.pallas_call(
        _inproj_small_kernel,
        out_shape=jax.ShapeDtypeStruct((m, LANES), F32),
        grid=(pl.cdiv(m, tm),),
        in_specs=[pl.BlockSpec((tm, d), lambda i: (i, 0)), pl.BlockSpec((d, LANES), lambda i: (0, 0))],
        out_specs=pl.BlockSpec((tm, LANES), lambda i: (i, 0)),
        compiler_params=_cparams(("arbitrary",)),
        name="inproj_small",
    )(h, w_small)


def _prep_w_in(w_in, d_model):
    mw = GROUP
    sizes = (mw, mw, mw, M_HEADS, M_HEADS, GROUP, GROUP, GROUP, IDX_HEADS * IDX_DIM, IDX_DIM,
             IDX_HEADS, N_BRANCHES * d_model)
    offs = [0]
    for s in sizes:
        offs.append(offs[-1] + s)
    u_m, v_m, o_m, i_m, f_m, q_a, k_a, v_a, q_i, k_i, w_i, gate = range(12)
    starts = [offs[gate] + r * GROUP for r in range(N_BRANCHES * d_model // GROUP)]
    starts += [offs[k] for k in (u_m, v_m, o_m, q_a, k_a, v_a, q_i)]
    assert len(starts) == N_BIG_BLOCKS and all(s % SUBLANES == 0 for s in starts)
    tab = jnp.array([s // SUBLANES for s in starts], I32)
    col = lambda k: w_in[:, offs[k]:offs[k + 1]]
    fill = jnp.zeros((w_in.shape[0], LANES - S_FGATE - M_HEADS), w_in.dtype)
    w_small = jnp.concatenate([col(k_i), col(w_i), col(i_m), col(f_m), fill], axis=1).astype(BF16)
    return jnp.swapaxes(w_in, 0, 1), w_small, tab


def _mlstm_prompt_kernel(u_ref, v_ref, o_ref, ps_ref, bif_ref, wconv_ref, bconv_ref, wqk_ref, gmo_ref,
                         hm_ref, conv_ref, cst_ref, nst_ref, mst_ref,
                         ext_ref, c_ref, n_ref, m_ref, *, rows, nsteps):
    i = pl.program_id(0)
    mw = u_ref.shape[1]
    hv = mw // M_HEADS
    hk = hv // 2
    L = M_CHUNK

    @pl.when(i == 0)
    def _():
        ext_ref[0:8, :] = jnp.zeros((8, mw), F32)
        c_ref[...] = jnp.zeros(c_ref.shape, F32)
        n_ref[...] = jnp.zeros(n_ref.shape, F32)
        m_ref[...] = jnp.zeros(m_ref.shape, F32)

    u = u_ref[...]
    ext_ref[8:8 + rows, :] = u
    wc = wconv_ref[...]
    cv = (ext_ref[5:5 + rows, :] * wc[0:1] + ext_ref[6:6 + rows, :] * wc[1:2]
          + ext_ref[7:7 + rows, :] * wc[2:3] + u * wc[3:4]) + bconv_ref[...]
    ext_ref[0:8, :] = u[rows - 8:rows, :]
    cb = (cv * _sigmoid(cv)).astype(BF16)

    g = ps_ref[...] + bif_ref[...]
    lf = _log_sigmoid(g)
    r_i = lax.broadcasted_iota(I32, (rows, rows), 0)
    c_i = lax.broadcasted_iota(I32, (rows, rows), 1)
    tri = jnp.where((r_i // L == c_i // L) & (c_i <= r_i), 1.0, 0.0).astype(F32)
    bcum = _dot_exact(tri, lf)
    g_t = g.T
    b_t = bcum.T
    causal = lax.broadcasted_iota(I32, (L, L), 1) <= lax.broadcasted_iota(I32, (L, L), 0)

    for h in range(M_HEADS):
        hs = slice(h * hv, (h + 1) * hv)
        qk = _dot(cb[:, hs], wqk_ref[h])
        q = qk[:, :hk] * (hk ** -0.5)
        k = qk[:, hk:]
        v = v_ref[:, hs]
        li_col = g[:, S_IGATE + h:S_IGATE + h + 1]
        b_col = bcum[:, S_FGATE + h:S_FGATE + h + 1]
        li_row = g_t[S_IGATE + h:S_IGATE + h + 1, :]
        b_row = b_t[S_FGATE + h:S_FGATE + h + 1, :]
        cmat = c_ref[h]
        nvec = n_ref[h:h + 1, :]
        m = m_ref[h:h + 1, 0:1]
        for c in range(rows // L):
            sl = slice(c * L, (c + 1) * L)
            qc, kc, vc = q[sl], k[sl], v[sl]
            bc, lic = b_col[sl], li_col[sl]
            br, lir = b_row[:, sl], li_row[:, sl]
            qcb, kcb, vcb = qc.astype(BF16), kc.astype(BF16), vc.astype(BF16)
            dmat = jnp.where(causal, (bc - br) + lir, -jnp.inf)
            inter = bc + m
            m_t = jnp.maximum(inter, jnp.max(dmat, axis=-1, keepdims=True))
            s_inter = jnp.exp(inter - m_t)
            s = _dot_nt(qcb, kcb) * jnp.exp(dmat - m_t)
            num = _dot(s.astype(BF16), vcb) + s_inter * _dot(qcb, cmat.astype(BF16))
            den = (jnp.sum(s, axis=-1, keepdims=True)
                   + s_inter * jnp.sum(qc * nvec, axis=-1, keepdims=True))
            hc = num / jnp.maximum(jnp.abs(den), jnp.exp(-m_t))
            hn = _rms(hc, gmo_ref[h:h + 1, :]) * _sigmoid(o_ref[sl, hs])
            hm_ref[sl, hs] = hn.astype(BF16)
            b_last = bc[L - 1:L, :]
            g_row = (b_last - br) + lir
            g_col = (b_last - bc) + lic
            m_new = jnp.maximum(b_last + m, jnp.max(g_row, axis=-1, keepdims=True))
            s_old = jnp.exp(b_last + m - m_new)
            kw = kc * jnp.exp(g_col - m_new)
            cmat = s_old * cmat + _dot(kw.T.astype(BF16), vcb)
            nvec = s_old * nvec + jnp.sum(kw, axis=0, keepdims=True)
            m = m_new
        c_ref[h] = cmat
        n_ref[h:h + 1, :] = nvec
        m_ref[h:h + 1, :] = jnp.broadcast_to(m, (1, LANES))

    @pl.when(i == nsteps - 1)
    def _():
        conv_ref[...] = ext_ref[8 + rows - (CONV_W - 1):8 + rows, :]
        cst_ref[...] = c_ref[...]
        nst_ref[...] = n_ref[...]
        mst_ref[...] = m_ref[...]


def _mlstm_prompt(p, ps, bif_row, w_conv, b_conv, wqk, gmo, *, rows):
    t = p.shape[0]
    mw = GROUP
    hv = mw // M_HEADS
    hk = hv // 2
    nsteps = t // rows
    const = lambda shape: pl.BlockSpec(shape, lambda i: (0,) * len(shape))
    return pl.pallas_call(
        functools.partial(_mlstm_prompt_kernel, rows=rows, nsteps=nsteps),
        out_shape=(jax.ShapeDtypeStruct((t, mw), BF16),
                   jax.ShapeDtypeStruct((CONV_W - 1, mw), F32),
                   jax.ShapeDtypeStruct((M_HEADS, hk, hv), F32),
                   jax.ShapeDtypeStruct((8, hk), F32),
                   jax.ShapeDtypeStruct((8, LANES), F32)),
        grid=(nsteps,),
        in_specs=[
            pl.BlockSpec((rows, mw), lambda i: (i, C_UM)),
            pl.BlockSpec((rows, mw), lambda i: (i, C_VM)),
            pl.BlockSpec((rows, mw), lambda i: (i, C_OM)),
            pl.BlockSpec((rows, LANES), lambda i: (i, 0)),
            const((1, LANES)), const((CONV_W, mw)), const((1, mw)),
            const((M_HEADS, hv, 2 * hk)), const((M_HEADS, hv)),
        ],
        out_specs=(pl.BlockSpec((rows, mw), lambda i: (i, 0)),
                   const((CONV_W - 1, mw)), const((M_HEADS, hk, hv)), const((8, hk)), const((8, LANES))),
        scratch_shapes=[pltpu.VMEM((rows + 8, mw), F32), pltpu.VMEM((M_HEADS, hk, hv), F32),
                        pltpu.VMEM((8, hk), F32), pltpu.VMEM((8, LANES), F32)],
        compiler_params=_cparams(("arbitrary",)),
        name="mlstm_prompt",
    )(p, p, p, ps, bif_row, w_conv, b_conv, wqk, gmo)


def _mlstm_sample_kernel(u_ref, v_ref, o_ref, ps_ref, sc_ref, c0_ref, n0_ref, m0_ref,
                         bif_ref, wconv_ref, bconv_ref, wqk_ref, gmo_ref,
                         hm_ref, conv_ref, c1_ref, n1_ref, m1_ref, *, gb):
    mw = u_ref.shape[1]
    hv = mw // M_HEADS
    hk = hv // 2
    u = u_ref[...]
    wc = wconv_ref[...]
    cv = (sc_ref[0] * wc[0:1] + sc_ref[1] * wc[1:2] + sc_ref[2] * wc[2:3] + u * wc[3:4]) + bconv_ref[...]
    conv_ref[0] = sc_ref[1]
    conv_ref[1] = sc_ref[2]
    conv_ref[2] = u
    cb = (cv * _sigmoid(cv)).astype(BF16)
    g = ps_ref[...] + bif_ref[...]
    lf = _log_sigmoid(g)
    for h in range(M_HEADS):
        hs = slice(h * hv, (h + 1) * hv)
        qk = _dot(cb[:, hs], wqk_ref[h])
        q = qk[:, :hk] * (hk ** -0.5)
        k = qk[:, hk:]
        fill = jnp.zeros((LANES - gb, hk), F32)
        q_t = jnp.concatenate([q, fill], axis=0).T
        k_t = jnp.concatenate([k, fill], axis=0).T
        v = v_ref[:, hs]
        for s in range(gb):
            li = g[s:s + 1, S_IGATE + h:S_IGATE + h + 1]
            b = lf[s:s + 1, S_FGATE + h:S_FGATE + h + 1]
            m0 = m0_ref[s:s + 1, h:h + 1]
            qr, kr, vr = q[s:s + 1], k[s:s + 1], v[s:s + 1]
            cmat = c0_ref[s, h]
            nrow = n0_ref[s, h:h + 1, :]
            inter = b + m0
            m_t = jnp.maximum(inter, li)
            s_inter = jnp.exp(inter - m_t)
            dw = jnp.exp(li - m_t)
            sqk = jnp.sum(qr * kr, axis=-1, keepdims=True) * dw
            qc = jnp.sum(q_t[:, s:s + 1] * cmat, axis=0, keepdims=True)
            num = sqk * vr + s_inter * qc
            den = sqk + s_inter * jnp.sum(qr * nrow, axis=-1, keepdims=True)
            hc = num / jnp.maximum(jnp.abs(den), jnp.exp(-m_t))
            hn = _rms(hc, gmo_ref[h:h + 1, :]) * _sigmoid(o_ref[s:s + 1, hs])
            hm_ref[s:s + 1, hs] = hn.astype(BF16)
            c1_ref[s, h] = s_inter * cmat + (dw * k_t[:, s:s + 1]) * vr
            n1_ref[s, h:h + 1, :] = s_inter * nrow + dw * kr
            m1_ref[s:s + 1, h:h + 1] = m_t


def _mlstm_sample(p, ps, conv_t, c0, n0, m0, bif_row, w_conv, b_conv, wqk, gmo, *, gb):
    db = p.shape[0]
    mw = GROUP
    hv = mw // M_HEADS
    hk = hv // 2
    const = lambda shape: pl.BlockSpec(shape, lambda i: (0,) * len(shape))
    return pl.pallas_call(
        functools.partial(_mlstm_sample_kernel, gb=gb),
        out_shape=(jax.ShapeDtypeStruct((db, mw), BF16),
                   jax.ShapeDtypeStruct((CONV_W - 1, db, mw), F32),
                   jax.ShapeDtypeStruct((db, M_HEADS, hk, hv), F32),
                   jax.ShapeDtypeStruct((db, M_HEADS, hk), F32),
                   jax.ShapeDtypeStruct((db, M_HEADS), F32)),
        grid=(db // gb,),
        in_specs=[
            pl.BlockSpec((gb, mw), lambda i: (i, C_UM)),
            pl.BlockSpec((gb, mw), lambda i: (i, C_VM)),
            pl.BlockSpec((gb, mw), lambda i: (i, C_OM)),
            pl.BlockSpec((gb, LANES), lambda i: (i, 0)),
            pl.BlockSpec((CONV_W - 1, gb, mw), lambda i: (0, i, 0)),
            pl.BlockSpec((gb, M_HEADS, hk, hv), lambda i: (i, 0, 0, 0)),
            pl.BlockSpec((gb, M_HEADS, hk), lambda i: (i, 0, 0)),
            pl.BlockSpec((gb, M_HEADS), lambda i: (i, 0)),
            const((1, LANES)), const((CONV_W, mw)), const((1, mw)),
            const((M_HEADS, hv, 2 * hk)), const((M_HEADS, hv)),
        ],
        out_specs=(pl.BlockSpec((gb, mw), lambda i: (i, 0)),
                   pl.BlockSpec((CONV_W - 1, gb, mw), lambda i: (0, i, 0)),
                   pl.BlockSpec((gb, M_HEADS, hk, hv), lambda i: (i, 0, 0, 0)),
                   pl.BlockSpec((gb, M_HEADS, hk), lambda i: (i, 0, 0)),
                   pl.BlockSpec((gb, M_HEADS), lambda i: (i, 0))),
        compiler_params=_cparams(("arbitrary",)),
        name="mlstm_sample",
    )(p, p, p, ps, conv_t, c0, n0, m0, bif_row, w_conv, b_conv, wqk, gmo)


def _dsa_prep_kernel(k_ref, v_ref, ps_ref, gk_ref, k32_ref, v32_ref, kb_ref, vb_ref, ki_ref, kmx_ref):
    hd = gk_ref.shape[1]

    @pl.when(pl.program_id(0) == 0)
    def _():
        kmx_ref[...] = jnp.zeros(kmx_ref.shape, F32)

    for h in range(A_HEADS):
        hs = slice(h * hd, (h + 1) * hd)
        kn = _rms(k_ref[:, hs], gk_ref[...])
        k32_ref[:, hs] = kn
        kb = kn.astype(BF16)
        kb_ref[:, hs] = kb
        n2 = jnp.max(jnp.sum(jnp.square(kb.astype(F32)), axis=-1, keepdims=True), axis=0, keepdims=True)
        kmx_ref[h:h + 1, :] = jnp.maximum(kmx_ref[h:h + 1, :], n2)
    v = v_ref[...]
    v32_ref[...] = v
    vb_ref[...] = v.astype(BF16)
    ki_ref[...] = ps_ref[...].astype(BF16)


def _dsa_prep(p, ps, gk, *, tm):
    t = p.shape[0]
    hd = gk.shape[1]
    row = lambda w, c: pl.BlockSpec((tm, w), lambda i: (i, c))
    return pl.pallas_call(
        _dsa_prep_kernel,
        out_shape=(jax.ShapeDtypeStruct((t, GROUP), F32), jax.ShapeDtypeStruct((t, GROUP), F32),
                   jax.ShapeDtypeStruct((t, GROUP), BF16), jax.ShapeDtypeStruct((t, GROUP), BF16),
                   jax.ShapeDtypeStruct((t, LANES), BF16), jax.ShapeDtypeStruct((A_HEADS, LANES), F32)),
        grid=(t // tm,),
        in_specs=[row(GROUP, C_KA), row(GROUP, C_VA), row(LANES, 0),
                  pl.BlockSpec((1, hd), lambda i: (0, 0))],
        out_specs=(row(GROUP, 0), row(GROUP, 0), row(GROUP, 0), row(GROUP, 0), row(LANES, 0),
                   pl.BlockSpec((A_HEADS, LANES), lambda i: (0, 0))),
        compiler_params=_cparams(("arbitrary",)),
        name="dsa_prep",
    )(p, p, ps, gk)


def _dsa_prompt_kernel(q_ref, qi_ref, ps_ref, gq_ref, kmx_ref, ki_ref, kb_ref, vb_ref,
                       ha_ref, key_ref, qit_ref, qn_ref, wt_ref, mx_ref, l_ref, acc_ref, *, tq, topk):
    i = pl.program_id(0)
    hd = gq_ref.shape[1]
    ck = tq
    nk = i + 1
    kf = float(topk)
    SUB = 8

    def fold(x, op):
        out = x[0:SUB]
        for r in range(1, ck // SUB):
            out = op(out, x[r * SUB:(r + 1) * SUB])
        return out

    qit_ref[...] = qi_ref[...].T.astype(BF16)
    wt_ref[...] = ps_ref[...].T * ((IDX_HEADS ** -0.5) * (IDX_DIM ** -0.5))
    qscale = (hd ** -0.5) * LOG2E
    for h in range(A_HEADS):
        hs = slice(h * hd, (h + 1) * hd)
        qn_ref[:, hs] = (_rms(q_ref[:, hs], gq_ref[...]) * qscale).astype(BF16)

    q_pos = i * tq + lax.broadcasted_iota(I32, (ck, tq), 1)
    k_in = lax.broadcasted_iota(I32, (ck, tq), 0)

    def score_body(c, carry):
        off = pl.multiple_of(c * ck, ck)
        ki = ki_ref[pl.ds(off, ck), 0:IDX_DIM]
        acc = jnp.zeros((ck, tq), F32)
        for h in range(IDX_HEADS):
            s = _dot(ki, qit_ref[h * IDX_DIM:(h + 1) * IDX_DIM, :])
            acc = acc + jnp.maximum(s, 0.0) * wt_ref[S_WIDX + h:S_WIDX + h + 1, :]
        key_ref[pl.ds(off, ck), :] = jnp.where(k_in + c * ck <= q_pos, _mono_key(acc), INT_MIN)
        return carry

    lax.fori_loop(0, nk, score_body, 0)

    def count_where(pred):
        def one(c, acc):
            off = pl.multiple_of(c * ck, ck)
            return acc + fold(jnp.where(pred(key_ref[pl.ds(off, ck), :], c * ck), 1.0, 0.0), jnp.add)
        acc = lax.fori_loop(0, nk // 2, lambda c2, a: one(2 * c2 + 1, one(2 * c2, a)), jnp.zeros((SUB, tq), F32))
        acc = lax.cond(nk % 2 == 1, lambda a: one(nk - 1, a), lambda a: a, acc)
        return jnp.sum(acc, axis=0, keepdims=True)

    def count_ge(cand):
        return count_where(lambda kc, base: kc >= cand)

    cnt0 = count_ge(jnp.zeros((1, tq), I32))
    ok0 = cnt0 >= kf
    thr0 = jnp.where(ok0, 0, INT_MIN).astype(I32)
    cnt_t0 = jnp.where(ok0, cnt0, kf)

    def bit_body(b, carry):
        thr, cnt_t = carry
        cand = thr + jnp.left_shift(jnp.int32(1), 30 - b)
        cnt = count_ge(cand)
        ok = cnt >= kf
        return jnp.where(ok, cand, thr), jnp.where(ok, cnt, cnt_t)

    thr, cnt_t = lax.fori_loop(0, 31, bit_body, (thr0, cnt_t0))
    has_tie = jnp.max(cnt_t) > kf
    bias_on, bias_off = 0.0, -jnp.inf

    @pl.when(jnp.logical_not(has_tie))
    def _():
        tmin = jnp.maximum(thr, INT_MIN + 1)

        def body(c, carry):
            off = pl.multiple_of(c * ck, ck)
            bias = jnp.where(key_ref[pl.ds(off, ck), :] >= tmin, bias_on, bias_off)
            key_ref[pl.ds(off, ck), :] = lax.bitcast_convert_type(bias.T, I32)
            return carry
        lax.fori_loop(0, nk, body, 0)

    @pl.when(has_tie)
    def _():
        n_gt = count_where(lambda kc, base: kc > thr)
        need = kf - n_gt

        def pos_body(b, x):
            cand = x + jnp.left_shift(jnp.int32(1), b)
            cnt = count_where(lambda kc, base: (kc == thr) & (k_in + base < cand))
            return jnp.where(cnt < need, cand, x)

        nbits = int(math.ceil(math.log2(key_ref.shape[0]))) + 1
        x = lax.fori_loop(0, nbits, lambda b, x: pos_body(nbits - 1 - b, x), jnp.zeros((1, tq), I32))
        x = jnp.where(thr > INT_MIN, x, -1)

        def body(c, carry):
            off = pl.multiple_of(c * ck, ck)
            kc = key_ref[pl.ds(off, ck), :]
            sel = (kc > thr) | ((kc == thr) & (k_in + c * ck <= x))
            key_ref[pl.ds(off, ck), :] = lax.bitcast_convert_type(jnp.where(sel, bias_on, bias_off).T, I32)
            return carry
        lax.fori_loop(0, nk, body, 0)

    ncol = ck // LANES

    def lane_fold(x, op):
        out = x[:, 0:LANES]
        for r in range(1, ncol):
            out = op(out, x[:, r * LANES:(r + 1) * LANES])
        return out

    def scores(c, h):
        off = pl.multiple_of(c * ck, ck)
        hs = slice(h * hd, (h + 1) * hd)
        bias = lax.bitcast_convert_type(key_ref[pl.ds(off, ck), :], F32)
        return _dot_nt(qn_ref[:, hs], kb_ref[pl.ds(off, ck), hs]) + bias

    def sum_pass():
        acc_ref[...] = jnp.zeros(acc_ref.shape, F32)
        l_ref[...] = jnp.zeros(l_ref.shape, F32)

        def sum_body(c, carry):
            off = pl.multiple_of(c * ck, ck)
            for h in range(A_HEADS):
                hs = slice(h * hd, (h + 1) * hd)
                p = jnp.exp2(scores(c, h) - jnp.tile(mx_ref[h], (1, ncol)))
                l_ref[h] += lane_fold(p, jnp.add)
                acc_ref[h] += _dot(p.astype(BF16), vb_ref[pl.ds(off, ck), hs])
            return carry

        lax.fori_loop(0, nk, sum_body, 0)
        return [jnp.sum(l_ref[h], axis=-1, keepdims=True) for h in range(A_HEADS)]

    for h in range(A_HEADS):
        qh = qn_ref[:, h * hd:(h + 1) * hd].astype(F32)
        bound = jnp.sqrt(jnp.sum(qh * qh, axis=-1, keepdims=True) * kmx_ref[h:h + 1, 0:1])
        mx_ref[h] = jnp.broadcast_to(bound, (tq, LANES))
    l_cols = sum_pass()
    l_min = l_cols[0]
    for h in range(1, A_HEADS):
        l_min = jnp.minimum(l_min, l_cols[h])

    def finish(l_cols):
        for h in range(A_HEADS):
            ha_ref[:, h * hd:(h + 1) * hd] = (acc_ref[h] / l_cols[h]).astype(BF16)

    finish(l_cols)

    @pl.when(jnp.logical_not(jnp.min(l_min) >= SOFTMAX_SUM_FLOOR))
    def _():
        mx_ref[...] = jnp.full(mx_ref.shape, -jnp.inf, F32)

        def max_body(c, carry):
            for h in range(A_HEADS):
                mx_ref[h] = jnp.maximum(mx_ref[h], lane_fold(scores(c, h), jnp.maximum))
            return carry

        lax.fori_loop(0, nk, max_body, 0)
        for h in range(A_HEADS):
            mx_ref[h] = jnp.broadcast_to(jnp.max(mx_ref[h], axis=-1, keepdims=True), (tq, LANES))
        finish(sum_pass())


def _dsa_prompt(p, ps, gq, kmx, ki, kb, vb, *, tq):
    t = p.shape[0]
    hd = gq.shape[1]
    topk = min(TOPK_MAX, t // 4)
    return pl.pallas_call(
        functools.partial(_dsa_prompt_kernel, tq=tq, topk=topk),
        out_shape=jax.ShapeDtypeStruct((t, GROUP), BF16),
        grid=(t // tq,),
        in_specs=[
            pl.BlockSpec((tq, GROUP), lambda i: (i, C_QA)),
            pl.BlockSpec((tq, GROUP), lambda i: (i, C_QI)),
            pl.BlockSpec((tq, LANES), lambda i: (i, 0)),
            pl.BlockSpec((1, hd), lambda i: (0, 0)),
            pl.BlockSpec((A_HEADS, LANES), lambda i: (0, 0)),
            _vmem_spec(), _vmem_spec(), _vmem_spec(),
        ],
        out_specs=pl.BlockSpec((tq, GROUP), lambda i: (i, 0)),
        scratch_shapes=[pltpu.VMEM((t, tq), I32),
                        pltpu.VMEM((IDX_HEADS * IDX_DIM, tq), BF16),
                        pltpu.VMEM((tq, GROUP), BF16),
                        pltpu.VMEM((LANES, tq), F32),
                        pltpu.VMEM((A_HEADS, tq, LANES), F32),
                        pltpu.VMEM((A_HEADS, tq, LANES), F32),
                        pltpu.VMEM((A_HEADS, tq, hd), F32)],
        compiler_params=_cparams(("arbitrary",)),
        name="dsa_prompt",
    )(p, p, ps, gq, kmx, ki, kb, vb)


def _dsa_sample_prep_kernel(q_ref, k_ref, v_ref, qi_ref, ps_ref, gq_ref, gk_ref,
                            qs_ref, kn_ref, vn_ref, sn_ref):
    hd = gq_ref.shape[1]
    qscale = (hd ** -0.5) * LOG2E
    for h in range(A_HEADS):
        hs = slice(h * hd, (h + 1) * hd)
        qs_ref[:, hs] = _rms(q_ref[:, hs], gq_ref[...]) * qscale
        kn_ref[:, hs] = _rms(k_ref[:, hs], gk_ref[...])
    vn_ref[...] = v_ref[...]
    ps = ps_ref[...]
    lane = lax.broadcasted_iota(I32, ps.shape, 1)
    ke = jnp.where(lane < IDX_DIM, ps, 0.0)
    ko = pltpu.roll(ke, IDX_DIM, axis=1)
    w_scale = (IDX_HEADS ** -0.5) * (IDX_DIM ** -0.5)
    acc = jnp.zeros((ps.shape[0], 1), F32)
    for pr in range(IDX_HEADS // 2):
        q2 = qi_ref[:, pr * LANES:(pr + 1) * LANES]
        se = jnp.sum(q2 * ke, axis=-1, keepdims=True)
        so = jnp.sum(q2 * ko, axis=-1, keepdims=True)
        acc = acc + jnp.maximum(se, 0.0) * (ps[:, S_WIDX + 2 * pr:S_WIDX + 2 * pr + 1] * w_scale)
        acc = acc + jnp.maximum(so, 0.0) * (ps[:, S_WIDX + 2 * pr + 1:S_WIDX + 2 * pr + 2] * w_scale)
    sn_ref[...] = jnp.broadcast_to(acc, sn_ref.shape)


def _dsa_sample_prep(p, ps, gq, gk):
    db = p.shape[0]
    hd = gq.shape[1]
    blk = lambda c: pl.BlockSpec((db, GROUP), lambda i: (0, c))
    full = lambda shape: pl.BlockSpec(shape, lambda i: (0,) * len(shape))
    return pl.pallas_call(
        _dsa_sample_prep_kernel,
        out_shape=(jax.ShapeDtypeStruct((db, GROUP), F32), jax.ShapeDtypeStruct((db, GROUP), F32),
                   jax.ShapeDtypeStruct((db, GROUP), F32), jax.ShapeDtypeStruct((db, LANES), F32)),
        grid=(1,),
        in_specs=[blk(C_QA), blk(C_KA), blk(C_VA), blk(C_QI), full((db, LANES)), full((1, hd)), full((1, hd))],
        out_specs=(full((db, GROUP)), full((db, GROUP)), full((db, GROUP)), full((db, LANES))),
        compiler_params=_cparams(("arbitrary",)),
        name="dsa_sample_prep",
    )(p, p, p, p, ps, gq, gk)


def _dsa_sample_scores_kernel(pt_ref, qi_ref, w_ref, ckt_ref, out_ref, pbuf, sem, *, n_pages, nseq):
    b = pl.program_id(0)
    slot = b % 2

    def page_copy(seq, p, sl):
        return pltpu.make_async_copy(ckt_ref.at[0, pt_ref[seq, p]], pbuf.at[sl, p], sem.at[sl])

    def issue(seq, sl):
        def body(p, carry):
            page_copy(seq, p, sl).start()
            return carry
        lax.fori_loop(0, n_pages, body, 0, unroll=8)

    @pl.when(b == 0)
    def _():
        issue(0, 0)

    @pl.when(b + 1 < nseq)
    def _():
        issue(b + 1, 1 - slot)

    def wait_body(p, carry):
        page_copy(b, p, slot).wait()
        return carry
    lax.fori_loop(0, n_pages, wait_body, 0, unroll=8)

    qi = qi_ref[0].astype(BF16)
    w = w_ref[0]
    two = 2 * PAGE_SIZE

    def score_body(p2, carry):
        kt = jnp.concatenate([pbuf[slot, 2 * p2], pbuf[slot, 2 * p2 + 1]], axis=1).astype(BF16)
        s = jnp.maximum(_dot(qi, kt), 0.0) * w
        out_ref[0, :, pl.ds(pl.multiple_of(p2 * two, two), two)] = jnp.sum(s, axis=0, keepdims=True)
        return carry
    lax.fori_loop(0, n_pages // 2, score_body, 0, unroll=8)


def _dsa_sample_scores(page_table, qi3, w3, cache_kidx_t):
    db, n_pages = page_table.shape
    assert n_pages % 16 == 0
    grid_spec = pltpu.PrefetchScalarGridSpec(
        num_scalar_prefetch=1,
        grid=(db,),
        in_specs=[pl.BlockSpec((1, IDX_HEADS, IDX_DIM), lambda b, pt: (b, 0, 0)),
                  pl.BlockSpec((1, IDX_HEADS, 1), lambda b, pt: (b, 0, 0)),
                  pl.BlockSpec(memory_space=pl.ANY)],
        out_specs=pl.BlockSpec((1, 1, n_pages * PAGE_SIZE), lambda b, pt: (b, 0, 0)),
        scratch_shapes=[pltpu.VMEM((2, n_pages, IDX_DIM, PAGE_SIZE), F32), pltpu.SemaphoreType.DMA((2,))],
    )
    return pl.pallas_call(
        functools.partial(_dsa_sample_scores_kernel, n_pages=n_pages, nseq=db),
        out_shape=jax.ShapeDtypeStruct((db, 1, n_pages * PAGE_SIZE), F32),
        grid_spec=grid_spec,
        compiler_params=_cparams(("arbitrary",)),
        name="dsa_sample_scores",
    )(page_table, qi3, w3, cache_kidx_t)


def _dsa_sample_select_kernel(sc_ref, sn_ref, ptt_ref, idx_ref, nv_ref, ns_ref, *, topk):
    db, npg, pgs = sc_ref.shape
    kf = float(topk)
    key = _mono_key(sc_ref[...])
    key_n = _mono_key(sn_ref[...][:, 0:1]).reshape(db, 1, 1)

    def total(x):
        return jnp.sum(jnp.sum(x, axis=2, keepdims=True), axis=1, keepdims=True)

    def count_ge(cand):
        return (total(jnp.where(key >= cand, 1.0, 0.0)) + jnp.where(key_n >= cand, 1.0, 0.0))

    cnt0 = count_ge(jnp.zeros((db, 1, 1), I32))
    thr0 = jnp.where(cnt0 >= kf, 0, INT_MIN).astype(I32)

    def bit_body(b, thr):
        cand = thr + jnp.left_shift(jnp.int32(1), 30 - b)
        return jnp.where(count_ge(cand) >= kf, cand, thr)

    thr = lax.fori_loop(0, 31, bit_body, thr0)
    gt = jnp.where(key > thr, 1.0, 0.0)
    eq = jnp.where(key == thr, 1.0, 0.0)
    gt_n = jnp.where(key_n > thr, 1.0, 0.0)
    eq_n = jnp.where(key_n == thr, 1.0, 0.0)
    need = kf - (total(gt) + gt_n)

    r_i = lax.broadcasted_iota(I32, (pgs, pgs), 0)
    c_i = lax.broadcasted_iota(I32, (pgs, pgs), 1)
    upper = jnp.where(r_i <= c_i, 1.0, 0.0).astype(BF16)
    lower = jnp.where(c_i <= r_i, 1.0, 0.0).astype(BF16)
    lower_pg = jnp.where(lax.broadcasted_iota(I32, (npg, npg), 1)
                         <= lax.broadcasted_iota(I32, (npg, npg), 0), 1.0, 0.0).astype(BF16)
    j_row = lax.broadcasted_iota(I32, (1, topk), 1).astype(F32)
    p_col = lax.broadcasted_iota(I32, (npg, 1), 0).astype(F32)

    def page_prefix(x):
        pin = _dot(x.astype(BF16), upper)
        rt = pin[:, pgs - 1:pgs]
        incl = _dot(lower_pg, jnp.broadcast_to(rt, (npg, pgs)).astype(BF16))[:, 0:1]
        return pin, rt, incl

    for b in range(db):
        eq_b = eq[b]
        pin_e, rt_e, incl_e = page_prefix(eq_b)
        rank_e = (incl_e - rt_e) + pin_e
        need_b = need[b]
        sel = gt[b] + jnp.where(rank_e <= need_b, eq_b, 0.0)
        n_eq = incl_e[npg - 1:npg, :]
        new_sel = gt_n[b] + jnp.where(n_eq + 1.0 <= need_b, eq_n[b], 0.0)

        pin, rt, incl = page_prefix(sel)
        ex = incl - rt
        n_sel = incl[npg - 1:npg, :]
        page_j = jnp.sum(jnp.where(incl <= j_row, 1.0, 0.0), axis=0, keepdims=True)
        oh_t = p_col == page_j
        ex_j = jnp.sum(jnp.where(oh_t, ex, 0.0), axis=0, keepdims=True)
        r_j = j_row - ex_j + 1.0
        pin_t = _dot(lower, sel.T.astype(BF16))
        rows_t = _dot(pin_t.astype(BF16), jnp.where(oh_t, 1.0, 0.0).astype(BF16))
        off_j = jnp.sum(jnp.where(rows_t < r_j, 1.0, 0.0), axis=0, keepdims=True)
        phys_j = jnp.sum(jnp.where(oh_t, ptt_ref[:, b:b + 1], 0.0), axis=0, keepdims=True)
        idx = jnp.where(j_row < n_sel, phys_j * float(pgs) + off_j, 0.0)
        idx_ref[b:b + 1, :] = idx.astype(I32)
        nv_ref[b:b + 1, :] = jnp.broadcast_to(n_sel, (1, LANES)).astype(I32)
        ns_ref[b:b + 1, :] = jnp.broadcast_to(new_sel, (1, LANES)).astype(I32)


def _dsa_sample_select(sc3, sn, pt_t, *, topk):
    db = sc3.shape[0]
    return pl.pallas_call(
        functools.partial(_dsa_sample_select_kernel, topk=topk),
        out_shape=(jax.ShapeDtypeStruct((db, topk), I32), jax.ShapeDtypeStruct((db, LANES), I32),
                   jax.ShapeDtypeStruct((db, LANES), I32)),
        in_specs=[_vmem_spec(), _vmem_spec(), _vmem_spec()],
        out_specs=(_vmem_spec(), _vmem_spec(), _vmem_spec()),
        compiler_params=_cparams(None),
        name="dsa_sample_select",
    )(sc3, sn, pt_t)


def _dsa_sample_attend_kernel(idx_ref, nv_ref, ns_ref, q_ref, kn_ref, vn_ref, ck_ref, cv_ref,
                              o_ref, kbuf, vbuf, sem, *, topk, nseq):
    b = pl.program_id(0)
    slot = b % 2
    nh = A_HEADS
    hd = q_ref.shape[2] // nh

    def row_copies(seq, j, sl):
        r = idx_ref[seq, j]
        dst = pl.ds(pl.multiple_of(j * nh, nh), nh)
        return (pltpu.make_async_copy(ck_ref.at[0, r], kbuf.at[sl, dst], sem.at[0, sl]),
                pltpu.make_async_copy(cv_ref.at[0, r], vbuf.at[sl, dst], sem.at[1, sl]))

    def issue(seq, sl):
        def body(j, carry):
            ck, cv = row_copies(seq, j, sl)
            ck.start()
            cv.start()
            return carry
        lax.fori_loop(0, topk, body, 0, unroll=8)

    @pl.when(b == 0)
    def _():
        issue(0, 0)

    @pl.when(b + 1 < nseq)
    def _():
        issue(b + 1, 1 - slot)

    def wait_body(j, carry):
        ck, cv = row_copies(b, j, slot)
        ck.wait()
        cv.wait()
        return carry
    lax.fori_loop(0, topk, wait_body, 0, unroll=8)

    n_valid = nv_ref[b]
    new_on = ns_ref[b] > 0
    j_row = lax.broadcasted_iota(I32, (1, topk), 1)
    for h in range(nh):
        hs = slice(h * hd, (h + 1) * hd)
        qh = q_ref[0, :, hs]
        kh = kbuf[slot, pl.ds(h, topk, stride=nh), :]
        vh = vbuf[slot, pl.ds(h, topk, stride=nh), :]
        s = _dot_nt(qh.astype(BF16), kh.astype(BF16))
        s = jnp.where(j_row < n_valid, s, -jnp.inf)
        s_new = jnp.sum(qh * kn_ref[0, :, hs], axis=-1, keepdims=True)
        s_new = jnp.where(new_on, s_new, -jnp.inf)
        m = jnp.maximum(jnp.max(s, axis=-1, keepdims=True), s_new)
        p = jnp.exp2(s - m)
        p_new = jnp.exp2(s_new - m)
        l = jnp.sum(p, axis=-1, keepdims=True) + p_new
        o = _dot(p.astype(BF16), vh.astype(BF16)) + p_new * vn_ref[0, :, hs]
        o_ref[0, :, hs] = o / l


def _dsa_sample_attend(idx, nv, ns, qs3, kn3, vn3, cache_k, cache_v, *, topk):
    db = idx.shape[0]
    w = qs3.shape[2]
    hd = w // A_HEADS
    row = pl.BlockSpec((1, 1, w), lambda b, *_: (b, 0, 0))
    grid_spec = pltpu.PrefetchScalarGridSpec(
        num_scalar_prefetch=3,
        grid=(db,),
        in_specs=[row, row, row, pl.BlockSpec(memory_space=pl.ANY), pl.BlockSpec(memory_space=pl.ANY)],
        out_specs=row,
        scratch_shapes=[pltpu.VMEM((2, topk * A_HEADS, hd), F32), pltpu.VMEM((2, topk * A_HEADS, hd), F32),
                        pltpu.SemaphoreType.DMA((2, 2))],
    )
    return pl.pallas_call(
        functools.partial(_dsa_sample_attend_kernel, topk=topk, nseq=db),
        out_shape=jax.ShapeDtypeStruct((db, 1, w), F32),
        grid_spec=grid_spec,
        compiler_params=_cparams(("arbitrary",)),
        name="dsa_sample_attend",
    )(idx, nv, ns, qs3, kn3, vn3, cache_k, cache_v)


def _merge_kernel(hm_ref, ha_ref, gm_ref, ga_ref, x_ref, wmo_ref, wao_ref, wout_ref, y_ref):
    t1 = _dot(hm_ref[...].astype(BF16), wmo_ref[...])
    t2 = _dot(ha_ref[...].astype(BF16), wao_ref[...])
    z = _sigmoid(gm_ref[...]) * t1 + _sigmoid(ga_ref[...]) * t2
    y_ref[...] = x_ref[...] + _dot(z.astype(BF16), wout_ref[...])


def _merge(hm, ha, p, x, wmo, wao, wout, *, tm):
    m, d = x.shape
    w = hm.shape[1]
    return pl.pallas_call(
        _merge_kernel,
        out_shape=jax.ShapeDtypeStruct((m, d), F32),
        grid=(pl.cdiv(m, tm),),
        in_specs=[
            pl.BlockSpec((tm, w), lambda i: (i, 0)),
            pl.BlockSpec((tm, w), lambda i: (i, 0)),
            pl.BlockSpec((tm, d), lambda i: (i, 0)),
            pl.BlockSpec((tm, d), lambda i: (i, 1)),
            pl.BlockSpec((tm, d), lambda i: (i, 0)),
            _vmem_spec(), _vmem_spec(), _vmem_spec(),
        ],
        out_specs=pl.BlockSpec((tm, d), lambda i: (i, 0)),
        compiler_params=_cparams(("arbitrary",)),
        name="merge",
    )(hm, ha, p, p, x, wmo, wao, wout)


FFN_TM = 512
FFN_TF = 512
PROJ_TM = 512
MERGE_TM = 256
MLSTM_ROWS = 256
MLSTM_SAMPLE_GROUP = 8
DSA_TQ = 256
DSA_PREP_TM = 512


def kernel(x_prompt, x_sample, cache_k, cache_v, cache_kidx, state_conv, state_C, state_n, state_m, page_table, norm_ffn1, w_ffn1_up, w_ffn1_down, norm_mix, w_in, b_if, w_conv, b_conv, w_qk_m, norm_mlstm_out, norm_q, norm_k, w_mo, w_ao, w_out, norm_ffn2, w_ffn2_up, w_ffn2_down):
    depth = w_in.shape[0]
    assert depth == 1, "single-layer step"
    bsz, seq, d = x_prompt.shape
    assert bsz == 1
    db, dseq, _ = x_sample.shape
    assert dseq == 1
    n_pages = page_table.shape[1]
    assert cache_k.shape[2] == PAGE_SIZE
    past = n_pages * PAGE_SIZE
    topk_s = min(TOPK_MAX, (past + dseq) // 4)
    l = 0

    xp = x_prompt.reshape(seq, d)
    xs = x_sample.reshape(db, d)

    wu1, wd1 = w_ffn1_up[l].astype(BF16), w_ffn1_down[l].astype(BF16)
    wu2, wd2 = w_ffn2_up[l].astype(BF16), w_ffn2_down[l].astype(BF16)
    w_t, w_small, tab = _prep_w_in(w_in[l], d)
    wmo, wao, wout = w_mo[l].astype(BF16), w_ao[l].astype(BF16), w_out[l].astype(BF16)
    wqk = w_qk_m[l].astype(BF16)
    g1, gmix, g2 = norm_ffn1[l][None], norm_mix[l][None], norm_ffn2[l][None]
    gq, gk = norm_q[l][None], norm_k[l][None]
    gmo = norm_mlstm_out[l]
    bif_row = jnp.zeros((1, LANES), F32).at[0, S_IGATE:S_IGATE + 2 * M_HEADS].set(b_if[l])
    wcv, bcv = w_conv[l], b_conv[l][None]

    xp1, hp = _ffn(xp, g1, wu1, wd1, tm=FFN_TM, tf=FFN_TF, g_next=gmix)
    xs1, hs = _ffn(xs, g1, wu1, wd1, tm=db, tf=FFN_TF, g_next=gmix)

    pp, pps = _inproj(hp, w_t, tab, tm=PROJ_TM), _inproj_small(hp, w_small, tm=PROJ_TM)
    sp, sps = _inproj(hs, w_t, tab, tm=db), _inproj_small(hs, w_small, tm=db)

    hm_p, conv_p, c_p, n_p, m_p = _mlstm_prompt(pp, pps, bif_row, wcv, bcv, wqk, gmo, rows=MLSTM_ROWS)
    k32, v32, kb, vb, ki, kmx = _dsa_prep(pp, pps, gk, tm=DSA_PREP_TM)
    ha_p = _dsa_prompt(pp, pps, gq, kmx, ki, kb, vb, tq=DSA_TQ)
    yp = _merge(hm_p, ha_p, pp, xp1, wmo, wao, wout, tm=MERGE_TM)
    out_p = _ffn(yp, g2, wu2, wd2, tm=FFN_TM, tf=FFN_TF)

    conv_t = jnp.transpose(state_conv[l], (1, 0, 2))
    hm_s, conv_s, c_s, n_s, m_s = _mlstm_sample(sp, sps, conv_t, state_C[l], state_n[l], state_m[l],
                                                bif_row, wcv, bcv, wqk, gmo, gb=MLSTM_SAMPLE_GROUP)
    qs, kn_s, vn_s, sn = _dsa_sample_prep(sp, sps, gq, gk)
    qi3 = sp[:, C_QI * GROUP:(C_QI + 1) * GROUP].reshape(db, IDX_HEADS, IDX_DIM)
    w3 = (sps[:, S_WIDX:S_WIDX + IDX_HEADS] * ((IDX_HEADS ** -0.5) * (IDX_DIM ** -0.5))).reshape(db, IDX_HEADS, 1)
    sc = _dsa_sample_scores(page_table, qi3, w3, jnp.swapaxes(cache_kidx, 2, 3))
    sc3 = sc.reshape(db, n_pages, PAGE_SIZE)
    idx, nv, ns = _dsa_sample_select(sc3, sn, page_table.T.astype(F32), topk=topk_s)
    n_rows = cache_k.shape[1] * PAGE_SIZE
    ha_s = _dsa_sample_attend(idx, nv[:, 0], ns[:, 0], qs.reshape(db, 1, GROUP),
                              kn_s.reshape(db, 1, GROUP), vn_s.reshape(db, 1, GROUP),
                              cache_k.reshape(depth, n_rows, A_HEADS, GROUP // A_HEADS),
                              cache_v.reshape(depth, n_rows, A_HEADS, GROUP // A_HEADS), topk=topk_s)
    ys = _merge(hm_s, ha_s.reshape(db, GROUP), sp, xs1, wmo, wao, wout, tm=db)
    out_s = _ffn(ys, g2, wu2, wd2, tm=db, tf=FFN_TF)

    hd = GROUP // A_HEADS
    hv = GROUP // M_HEADS
    hk = hv // 2
    return (
        out_p.reshape(bsz, seq, d),
        out_s.reshape(db, dseq, d),
        k32.reshape(depth, bsz, seq, A_HEADS, hd),
        v32.reshape(depth, bsz, seq, A_HEADS, hd),
        pps[:, S_KIDX:S_KIDX + IDX_DIM].reshape(depth, bsz, seq, IDX_DIM),
        conv_p.reshape(depth, bsz, CONV_W - 1, GROUP),
        c_p.reshape(depth, bsz, M_HEADS, hk, hv),
        n_p[:M_HEADS].reshape(depth, bsz, M_HEADS, hk),
        m_p[:M_HEADS, 0].reshape(depth, bsz, M_HEADS),
        kn_s.reshape(depth, db, dseq, A_HEADS, hd),
        vn_s.reshape(depth, db, dseq, A_HEADS, hd),
        sps[:, S_KIDX:S_KIDX + IDX_DIM].reshape(depth, db, dseq, IDX_DIM),
        jnp.transpose(conv_s, (1, 0, 2)).reshape(depth, db, CONV_W - 1, GROUP),
        c_s.reshape(depth, db, M_HEADS, hk, hv),
        n_s.reshape(depth, db, M_HEADS, hk),
        m_s.reshape(depth, db, M_HEADS),
    )
```

```python
import functools
import math

import jax
import jax.numpy as jnp
from jax import lax
from jax.experimental import pallas as pl
from jax.experimental.pallas import tpu as pltpu

F32 = jnp.float32
BF16 = jnp.bfloat16
I32 = jnp.int32

EPS = 1e-6
LANES = 128
PAGE_SIZE = 128
M_HEADS = 4
M_CHUNK = 64
CONV_W = 4
A_HEADS = 8
IDX_HEADS = 16
IDX_DIM = 64
TOPK_MAX = 256
N_BRANCHES = 2
GROUP = 1024
INT_MIN = -2 ** 31
NEG_INIT = -1e30
LOG2E = 1.4426950408889634
SOFTMAX_SUM_FLOOR = 2.0 ** -100
VMEM_LIMIT = 56 * 1024 * 1024

S_KIDX = 0
S_WIDX = IDX_DIM
S_IGATE = IDX_DIM + IDX_HEADS
S_FGATE = S_IGATE + M_HEADS

C_GATE, C_UM, C_VM, C_OM, C_QA, C_KA, C_VA, C_QI = 0, 4, 5, 6, 7, 8, 9, 10
N_BIG_BLOCKS = 11


def _cparams(sem, vmem=VMEM_LIMIT):
    return pltpu.CompilerParams(dimension_semantics=sem, vmem_limit_bytes=vmem)


def _vmem_spec():
    return pl.BlockSpec(memory_space=pltpu.VMEM)


def _rms(x, g):
    return x * lax.rsqrt(jnp.mean(x * x, axis=-1, keepdims=True) + EPS) * g


def _sigmoid(x):
    return 1.0 / (1.0 + jnp.exp(-x))


def _log_sigmoid(x):
    return jnp.minimum(x, 0.0) - jnp.log(1.0 + jnp.exp(-jnp.abs(x)))


def _dot(a, b):
    return jnp.dot(a, b, preferred_element_type=F32)


def _dot_nt(a, b):
    return lax.dot_general(a, b, (((1,), (1,)), ((), ())), preferred_element_type=F32)


def _dot_exact(a, b):
    return jnp.dot(a, b, preferred_element_type=F32, precision=lax.Precision.HIGHEST)


def _mono_key(x):
    b = lax.bitcast_convert_type(x, I32)
    return b ^ ((b >> 31) & jnp.int32(0x7FFFFFFF))


def _ffn_kernel(x_ref, g_ref, gn_ref, wa_ref, wb_ref, wd_ref, wat_ref, wbt_ref, wdt_ref, o_ref, *rest,
                nfull, emit_next):
    hn_ref, h_ref = rest if emit_next else (None, rest[0])
    j = pl.program_id(1)

    @pl.when(j == 0)
    def _():
        h_ref[...] = _rms(x_ref[...], g_ref[...]).astype(BF16)

    def half_step(wa, wb, wd):
        h = h_ref[...]
        a = _dot(h, wa[...])
        act = (a * _sigmoid(a) * _dot(h, wb[...])).astype(BF16)
        return _dot(act, wd[...])

    @pl.when(j == 0)
    def _():
        o_ref[...] = half_step(wa_ref, wb_ref, wd_ref)

    @pl.when((j > 0) & (j < nfull))
    def _():
        o_ref[...] += half_step(wa_ref, wb_ref, wd_ref)

    @pl.when(j == nfull)
    def _():
        y = x_ref[...] + 0.5 * (o_ref[...] + half_step(wat_ref, wbt_ref, wdt_ref))
        o_ref[...] = y
        if emit_next:
            hn_ref[...] = _rms(y, gn_ref[...]).astype(BF16)


def _ffn(x, g, w_up, w_down, *, tm, tf, g_next=None):
    emit_next = g_next is not None
    m, d = x.shape
    d_ff = w_down.shape[0]
    nfull = d_ff // tf
    tail = d_ff - nfull * tf
    assert tail > 0 and tail % LANES == 0 and d_ff % LANES == 0
    elem = lambda r, c: (pl.Element(r), pl.Element(c))
    main = lambda j: jnp.minimum(j, nfull - 1) * tf
    main_b = lambda j: (d_ff // LANES + jnp.minimum(j, nfull - 1) * (tf // LANES)) * LANES
    tail_b = (d_ff + nfull * tf) // LANES
    row_out = pl.BlockSpec((tm, d), lambda i, j: (i, 0))
    y_shape = jax.ShapeDtypeStruct((m, d), F32)
    return pl.pallas_call(
        functools.partial(_ffn_kernel, nfull=nfull, emit_next=emit_next),
        out_shape=(y_shape, jax.ShapeDtypeStruct((m, d), BF16)) if emit_next else y_shape,
        grid=(pl.cdiv(m, tm), nfull + 1),
        in_specs=[
            pl.BlockSpec((tm, d), lambda i, j: (i, 0)),
            pl.BlockSpec((1, d), lambda i, j: (0, 0)),
            pl.BlockSpec((1, d), lambda i, j: (0, 0)),
            pl.BlockSpec(elem(d, tf), lambda i, j: (0, main(j))),
            pl.BlockSpec(elem(d, tf), lambda i, j: (0, main_b(j))),
            pl.BlockSpec(elem(tf, d), lambda i, j: (main(j), 0)),
            pl.BlockSpec(elem(d, tail), lambda i, j: (0, nfull * tf)),
            pl.BlockSpec(elem(d, tail), lambda i, j: (0, tail_b * LANES)),
            pl.BlockSpec(elem(tail, d), lambda i, j: (nfull * tf, 0)),
        ],
        out_specs=(row_out, row_out) if emit_next else row_out,
        scratch_shapes=[pltpu.VMEM((tm, d), BF16)],
        compiler_params=_cparams(("arbitrary", "arbitrary")),
        name="ffn",
    )(x, g, g_next if emit_next else g, w_up, w_up, w_down, w_up, w_up, w_down)


SUBLANES = 8


def _inproj_kernel(tab_ref, h_ref, wt_ref, p_ref, w_ref):
    @pl.when(pl.program_id(1) == 0)
    def _():
        w_ref[...] = wt_ref[...].T.astype(BF16)

    p_ref[...] = _dot(h_ref[...], w_ref[...])


def _inproj(h, w_t, tab, *, tm):
    m, d = h.shape
    n_groups = tab.shape[0]
    grid_spec = pltpu.PrefetchScalarGridSpec(
        num_scalar_prefetch=1,
        grid=(n_groups, pl.cdiv(m, tm)),
        in_specs=[
            pl.BlockSpec((tm, d), lambda g, i, tab: (i, 0)),
            pl.BlockSpec((pl.Element(GROUP), pl.Element(d)), lambda g, i, tab: (tab[g] * SUBLANES, 0)),
        ],
        out_specs=pl.BlockSpec((tm, GROUP), lambda g, i, tab: (i, g)),
        scratch_shapes=[pltpu.VMEM((d, GROUP), BF16)],
    )
    return pl.pallas_call(
        _inproj_kernel,
        out_shape=jax.ShapeDtypeStruct((m, n_groups * GROUP), F32),
        grid_spec=grid_spec,
        compiler_params=_cparams(("arbitrary", "arbitrary")),
        name="inproj",
    )(tab, h, w_t)


def _inproj_small_kernel(h_ref, wst_ref, ps_ref):
    ps_ref[...] = _dot(h_ref[...], wst_ref[...].T.astype(BF16))


def _inproj_small(h, w_small_t, *, tm):
    m, d = h.shape
    return pl.pallas_call(
        _inproj_small_kernel,
        out_shape=jax.ShapeDtypeStruct((m, LANES), F32),
        grid=(pl.cdiv(m, tm),),
        in_specs=[pl.BlockSpec((tm, d), lambda i: (i, 0)), pl.BlockSpec((LANES, d), lambda i: (0, 0))],
        out_specs=pl.BlockSpec((tm, LANES), lambda i: (i, 0)),
        compiler_params=_cparams(("arbitrary",)),
        name="inproj_small",
    )(h, w_small_t)


def _prep_w_in(w_in, d_model):
    mw = GROUP
    sizes = (mw, mw, mw, M_HEADS, M_HEADS, GROUP, GROUP, GROUP, IDX_HEADS * IDX_DIM, IDX_DIM,
             IDX_HEADS, N_BRANCHES * d_model)
    offs = [0]
    for s in sizes:
        offs.append(offs[-1] + s)
    u_m, v_m, o_m, i_m, f_m, q_a, k_a, v_a, q_i, k_i, w_i, gate = range(12)
    starts = [offs[gate] + r * GROUP for r in range(N_BRANCHES * d_model // GROUP)]
    starts += [offs[k] for k in (u_m, v_m, o_m, q_a, k_a, v_a, q_i)]
    assert len(starts) == N_BIG_BLOCKS and all(s % SUBLANES == 0 for s in starts)
    tab = jnp.array([s // SUBLANES for s in starts], I32)
    w_t = jnp.swapaxes(w_in, 0, 1)
    rows = lambda k: w_t[offs[k]:offs[k + 1]]
    fill = jnp.zeros((LANES - S_FGATE - M_HEADS, w_in.shape[0]), w_in.dtype)
    w_small_t = jnp.concatenate([rows(k_i), rows(w_i), rows(i_m), rows(f_m), fill], axis=0)
    return w_t, w_small_t, tab


def _mlstm_prompt_kernel(u_ref, v_ref, o_ref, ps_ref, bif_ref, wconv_ref, bconv_ref, wqk_ref, gmo_ref,
                         hm_ref, conv_ref, cst_ref, nst_ref, mst_ref,
                         ext_ref, c_ref, n_ref, m_ref, *, rows, nsteps):
    i = pl.program_id(0)
    mw = u_ref.shape[1]
    hv = mw // M_HEADS
    hk = hv // 2
    L = M_CHUNK

    @pl.when(i == 0)
    def _():
        ext_ref[0:8, :] = jnp.zeros((8, mw), F32)
        c_ref[...] = jnp.zeros(c_ref.shape, F32)
        n_ref[...] = jnp.zeros(n_ref.shape, F32)
        m_ref[...] = jnp.zeros(m_ref.shape, F32)

    u = u_ref[...]
    ext_ref[8:8 + rows, :] = u
    wc = wconv_ref[...]
    cv = (ext_ref[5:5 + rows, :] * wc[0:1] + ext_ref[6:6 + rows, :] * wc[1:2]
          + ext_ref[7:7 + rows, :] * wc[2:3] + u * wc[3:4]) + bconv_ref[...]
    ext_ref[0:8, :] = u[rows - 8:rows, :]
    cb = (cv * _sigmoid(cv)).astype(BF16)

    g = ps_ref[...] + bif_ref[...]
    lf = _log_sigmoid(g)
    r_i = lax.broadcasted_iota(I32, (rows, rows), 0)
    c_i = lax.broadcasted_iota(I32, (rows, rows), 1)
    tri = jnp.where((r_i // L == c_i // L) & (c_i <= r_i), 1.0, 0.0).astype(F32)
    bcum = _dot_exact(tri, lf)
    g_t = g.T
    b_t = bcum.T
    causal = lax.broadcasted_iota(I32, (L, L), 1) <= lax.broadcasted_iota(I32, (L, L), 0)

    for h in range(M_HEADS):
        hs = slice(h * hv, (h + 1) * hv)
        qk = _dot(cb[:, hs], wqk_ref[h])
        q = qk[:, :hk] * (hk ** -0.5)
        k = qk[:, hk:]
        v = v_ref[:, hs]
        li_col = g[:, S_IGATE + h:S_IGATE + h + 1]
        b_col = bcum[:, S_FGATE + h:S_FGATE + h + 1]
        li_row = g_t[S_IGATE + h:S_IGATE + h + 1, :]
        b_row = b_t[S_FGATE + h:S_FGATE + h + 1, :]
        cmat = c_ref[h]
        nvec = n_ref[h:h + 1, :]
        m = m_ref[h:h + 1, 0:1]
        for c in range(rows // L):
            sl = slice(c * L, (c + 1) * L)
            qc, kc, vc = q[sl], k[sl], v[sl]
            bc, lic = b_col[sl], li_col[sl]
            br, lir = b_row[:, sl], li_row[:, sl]
            qcb, kcb, vcb = qc.astype(BF16), kc.astype(BF16), vc.astype(BF16)
            dmat = jnp.where(causal, (bc - br) + lir, -jnp.inf)
            inter = bc + m
            m_t = jnp.maximum(inter, jnp.max(dmat, axis=-1, keepdims=True))
            s_inter = jnp.exp(inter - m_t)
            s = _dot_nt(qcb, kcb) * jnp.exp(dmat - m_t)
            num = _dot(s.astype(BF16), vcb) + s_inter * _dot(qcb, cmat.astype(BF16))
            den = (jnp.sum(s, axis=-1, keepdims=True)
                   + s_inter * jnp.sum(qc * nvec, axis=-1, keepdims=True))
            hc = num / jnp.maximum(jnp.abs(den), jnp.exp(-m_t))
            hn = _rms(hc, gmo_ref[h:h + 1, :]) * _sigmoid(o_ref[sl, hs])
            hm_ref[sl, hs] = hn.astype(BF16)
            b_last = bc[L - 1:L, :]
            g_row = (b_last - br) + lir
            g_col = (b_last - bc) + lic
            m_new = jnp.maximum(b_last + m, jnp.max(g_row, axis=-1, keepdims=True))
            s_old = jnp.exp(b_last + m - m_new)
            kw = kc * jnp.exp(g_col - m_new)
            cmat = s_old * cmat + _dot(kw.T.astype(BF16), vcb)
            nvec = s_old * nvec + jnp.sum(kw, axis=0, keepdims=True)
            m = m_new
        c_ref[h] = cmat
        n_ref[h:h + 1, :] = nvec
        m_ref[h:h + 1, :] = jnp.broadcast_to(m, (1, LANES))

    @pl.when(i == nsteps - 1)
    def _():
        conv_ref[...] = ext_ref[8 + rows - (CONV_W - 1):8 + rows, :]
        cst_ref[...] = c_ref[...]
        nst_ref[...] = n_ref[...]
        mst_ref[...] = m_ref[...]


def _mlstm_prompt(p, ps, bif_row, w_conv, b_conv, wqk, gmo, *, rows):
    t = p.shape[0]
    mw = GROUP
    hv = mw // M_HEADS
    hk = hv // 2
    nsteps = t // rows
    const = lambda shape: pl.BlockSpec(shape, lambda i: (0,) * len(shape))
    return pl.pallas_call(
        functools.partial(_mlstm_prompt_kernel, rows=rows, nsteps=nsteps),
        out_shape=(jax.ShapeDtypeStruct((t, mw), BF16),
                   jax.ShapeDtypeStruct((CONV_W - 1, mw), F32),
                   jax.ShapeDtypeStruct((M_HEADS, hk, hv), F32),
                   jax.ShapeDtypeStruct((8, hk), F32),
                   jax.ShapeDtypeStruct((8, LANES), F32)),
        grid=(nsteps,),
        in_specs=[
            pl.BlockSpec((rows, mw), lambda i: (i, C_UM)),
            pl.BlockSpec((rows, mw), lambda i: (i, C_VM)),
            pl.BlockSpec((rows, mw), lambda i: (i, C_OM)),
            pl.BlockSpec((rows, LANES), lambda i: (i, 0)),
            const((1, LANES)), const((CONV_W, mw)), const((1, mw)),
            const((M_HEADS, hv, 2 * hk)), const((M_HEADS, hv)),
        ],
        out_specs=(pl.BlockSpec((rows, mw), lambda i: (i, 0)),
                   const((CONV_W - 1, mw)), const((M_HEADS, hk, hv)), const((8, hk)), const((8, LANES))),
        scratch_shapes=[pltpu.VMEM((rows + 8, mw), F32), pltpu.VMEM((M_HEADS, hk, hv), F32),
                        pltpu.VMEM((8, hk), F32), pltpu.VMEM((8, LANES), F32)],
        compiler_params=_cparams(("arbitrary",)),
        name="mlstm_prompt",
    )(p, p, p, ps, bif_row, w_conv, b_conv, wqk, gmo)


def _mlstm_sample_kernel(u_ref, v_ref, o_ref, ps_ref, sc_ref, c0_ref, n0_ref, m0_ref,
                         bif_ref, wconv_ref, bconv_ref, wqk_ref, gmo_ref,
                         hm_ref, conv_ref, c1_ref, n1_ref, m1_ref, *, gb):
    mw = u_ref.shape[1]
    hv = mw // M_HEADS
    hk = hv // 2
    u = u_ref[...]
    wc = wconv_ref[...]
    cv = (sc_ref[0] * wc[0:1] + sc_ref[1] * wc[1:2] + sc_ref[2] * wc[2:3] + u * wc[3:4]) + bconv_ref[...]
    conv_ref[0] = sc_ref[1]
    conv_ref[1] = sc_ref[2]
    conv_ref[2] = u
    cb = (cv * _sigmoid(cv)).astype(BF16)
    g = ps_ref[...] + bif_ref[...]
    lf = _log_sigmoid(g)
    for h in range(M_HEADS):
        hs = slice(h * hv, (h + 1) * hv)
        qk = _dot(cb[:, hs], wqk_ref[h])
        q = qk[:, :hk] * (hk ** -0.5)
        k = qk[:, hk:]
        fill = jnp.zeros((LANES - gb, hk), F32)
        q_t = jnp.concatenate([q, fill], axis=0).T
        k_t = jnp.concatenate([k, fill], axis=0).T
        v = v_ref[:, hs]
        for s in range(gb):
            li = g[s:s + 1, S_IGATE + h:S_IGATE + h + 1]
            b = lf[s:s + 1, S_FGATE + h:S_FGATE + h + 1]
            m0 = m0_ref[s:s + 1, h:h + 1]
            qr, kr, vr = q[s:s + 1], k[s:s + 1], v[s:s + 1]
            cmat = c0_ref[s, h]
            nrow = n0_ref[s, h:h + 1, :]
            inter = b + m0
            m_t = jnp.maximum(inter, li)
            s_inter = jnp.exp(inter - m_t)
            dw = jnp.exp(li - m_t)
            sqk = jnp.sum(qr * kr, axis=-1, keepdims=True) * dw
            qc = jnp.sum(q_t[:, s:s + 1] * cmat, axis=0, keepdims=True)
            num = sqk * vr + s_inter * qc
            den = sqk + s_inter * jnp.sum(qr * nrow, axis=-1, keepdims=True)
            hc = num / jnp.maximum(jnp.abs(den), jnp.exp(-m_t))
            hn = _rms(hc, gmo_ref[h:h + 1, :]) * _sigmoid(o_ref[s:s + 1, hs])
            hm_ref[s:s + 1, hs] = hn.astype(BF16)
            c1_ref[s, h] = s_inter * cmat + (dw * k_t[:, s:s + 1]) * vr
            n1_ref[s, h:h + 1, :] = s_inter * nrow + dw * kr
            m1_ref[s:s + 1, h:h + 1] = m_t


def _mlstm_sample(p, ps, conv_t, c0, n0, m0, bif_row, w_conv, b_conv, wqk, gmo, *, gb):
    db = p.shape[0]
    mw = GROUP
    hv = mw // M_HEADS
    hk = hv // 2
    const = lambda shape: pl.BlockSpec(shape, lambda i: (0,) * len(shape))
    return pl.pallas_call(
        functools.partial(_mlstm_sample_kernel, gb=gb),
        out_shape=(jax.ShapeDtypeStruct((db, mw), BF16),
                   jax.ShapeDtypeStruct((CONV_W - 1, db, mw), F32),
                   jax.ShapeDtypeStruct((db, M_HEADS, hk, hv), F32),
                   jax.ShapeDtypeStruct((db, M_HEADS, hk), F32),
                   jax.ShapeDtypeStruct((db, M_HEADS), F32)),
        grid=(db // gb,),
        in_specs=[
            pl.BlockSpec((gb, mw), lambda i: (i, C_UM)),
            pl.BlockSpec((gb, mw), lambda i: (i, C_VM)),
            pl.BlockSpec((gb, mw), lambda i: (i, C_OM)),
            pl.BlockSpec((gb, LANES), lambda i: (i, 0)),
            pl.BlockSpec((CONV_W - 1, gb, mw), lambda i: (0, i, 0)),
            pl.BlockSpec((gb, M_HEADS, hk, hv), lambda i: (i, 0, 0, 0)),
            pl.BlockSpec((gb, M_HEADS, hk), lambda i: (i, 0, 0)),
            pl.BlockSpec((gb, M_HEADS), lambda i: (i, 0)),
            const((1, LANES)), const((CONV_W, mw)), const((1, mw)),
            const((M_HEADS, hv, 2 * hk)), const((M_HEADS, hv)),
        ],
        out_specs=(pl.BlockSpec((gb, mw), lambda i: (i, 0)),
                   pl.BlockSpec((CONV_W - 1, gb, mw), lambda i: (0, i, 0)),
                   pl.BlockSpec((gb, M_HEADS, hk, hv), lambda i: (i, 0, 0, 0)),
                   pl.BlockSpec((gb, M_HEADS, hk), lambda i: (i, 0, 0)),
                   pl.BlockSpec((gb, M_HEADS), lambda i: (i, 0))),
        compiler_params=_cparams(("arbitrary",)),
        name="mlstm_sample",
    )(p, p, p, ps, conv_t, c0, n0, m0, bif_row, w_conv, b_conv, wqk, gmo)


def _dsa_prep_kernel(k_ref, v_ref, ps_ref, gk_ref, k32_ref, v32_ref, kb_ref, vb_ref, ki_ref, kmx_ref):
    hd = gk_ref.shape[1]

    @pl.when(pl.program_id(0) == 0)
    def _():
        kmx_ref[...] = jnp.zeros(kmx_ref.shape, F32)

    for h in range(A_HEADS):
        hs = slice(h * hd, (h + 1) * hd)
        kn = _rms(k_ref[:, hs], gk_ref[...])
        k32_ref[:, hs] = kn
        kb = kn.astype(BF16)
        kb_ref[:, hs] = kb
        n2 = jnp.max(jnp.sum(jnp.square(kb.astype(F32)), axis=-1, keepdims=True), axis=0, keepdims=True)
        kmx_ref[h:h + 1, :] = jnp.maximum(kmx_ref[h:h + 1, :], n2)
    v = v_ref[...]
    v32_ref[...] = v
    vb_ref[...] = v.astype(BF16)
    ki_ref[...] = ps_ref[...].astype(BF16)


def _dsa_prep(p, ps, gk, *, tm):
    t = p.shape[0]
    hd = gk.shape[1]
    row = lambda w, c: pl.BlockSpec((tm, w), lambda i: (i, c))
    return pl.pallas_call(
        _dsa_prep_kernel,
        out_shape=(jax.ShapeDtypeStruct((t, GROUP), F32), jax.ShapeDtypeStruct((t, GROUP), F32),
                   jax.ShapeDtypeStruct((t, GROUP), BF16), jax.ShapeDtypeStruct((t, GROUP), BF16),
                   jax.ShapeDtypeStruct((t, LANES), BF16), jax.ShapeDtypeStruct((A_HEADS, LANES), F32)),
        grid=(t // tm,),
        in_specs=[row(GROUP, C_KA), row(GROUP, C_VA), row(LANES, 0),
                  pl.BlockSpec((1, hd), lambda i: (0, 0))],
        out_specs=(row(GROUP, 0), row(GROUP, 0), row(GROUP, 0), row(GROUP, 0), row(LANES, 0),
                   pl.BlockSpec((A_HEADS, LANES), lambda i: (0, 0))),
        compiler_params=_cparams(("arbitrary",)),
        name="dsa_prep",
    )(p, p, ps, gk)


def _dsa_prompt_kernel(q_ref, qi_ref, ps_ref, gq_ref, kmx_ref, ki_ref, kb_ref, vb_ref,
                       ha_ref, key_ref, qit_ref, qn_ref, wt_ref, mx_ref, l_ref, acc_ref, *, tq, topk):
    i = pl.program_id(0)
    hd = gq_ref.shape[1]
    ck = tq
    nk = i + 1
    kf = float(topk)
    SUB = 8

    def fold(x, op):
        out = x[0:SUB]
        for r in range(1, ck // SUB):
            out = op(out, x[r * SUB:(r + 1) * SUB])
        return out

    qit_ref[...] = qi_ref[...].T.astype(BF16)
    wt_ref[...] = ps_ref[...].T * ((IDX_HEADS ** -0.5) * (IDX_DIM ** -0.5))
    qscale = (hd ** -0.5) * LOG2E
    for h in range(A_HEADS):
        hs = slice(h * hd, (h + 1) * hd)
        qn_ref[:, hs] = (_rms(q_ref[:, hs], gq_ref[...]) * qscale).astype(BF16)

    q_pos = i * tq + lax.broadcasted_iota(I32, (ck, tq), 1)
    k_in = lax.broadcasted_iota(I32, (ck, tq), 0)

    def for_chunks(body, width=CHUNKS_PER_TRIP):
        def trip(t, carry):
            for r in range(width):
                body(width * t + r)
            return carry
        lax.fori_loop(0, nk // width, trip, 0)

        def single(c, carry):
            body(c)
            return carry
        lax.fori_loop(nk - nk % width, nk, single, 0)

    def score_chunk(c):
        off = pl.multiple_of(c * ck, ck)
        ki = ki_ref[pl.ds(off, ck), 0:IDX_DIM]
        acc = jnp.zeros((ck, tq), F32)
        for h in range(IDX_HEADS):
            s = _dot(ki, qit_ref[h * IDX_DIM:(h + 1) * IDX_DIM, :])
            acc = acc + jnp.maximum(s, 0.0) * wt_ref[S_WIDX + h:S_WIDX + h + 1, :]
        key_ref[pl.ds(off, ck), :] = jnp.where(k_in + c * ck <= q_pos, _mono_key(acc), INT_MIN)

    for_chunks(score_chunk)

    def count_where(pred):
        def one(c, acc):
            off = pl.multiple_of(c * ck, ck)
            return acc + fold(jnp.where(pred(key_ref[pl.ds(off, ck), :], c * ck), 1.0, 0.0), jnp.add)
        acc = lax.fori_loop(0, nk // 2, lambda c2, a: one(2 * c2 + 1, one(2 * c2, a)), jnp.zeros((SUB, tq), F32))
        acc = lax.cond(nk % 2 == 1, lambda a: one(nk - 1, a), lambda a: a, acc)
        return jnp.sum(acc, axis=0, keepdims=True)

    def count_ge(cand):
        return count_where(lambda kc, base: kc >= cand)

    cnt0 = count_ge(jnp.zeros((1, tq), I32))
    ok0 = cnt0 >= kf
    thr0 = jnp.where(ok0, 0, INT_MIN).astype(I32)
    cnt_t0 = jnp.where(ok0, cnt0, kf)

    def bit_body(b, carry):
        thr, cnt_t = carry
        cand = thr + jnp.left_shift(jnp.int32(1), 30 - b)
        cnt = count_ge(cand)
        ok = cnt >= kf
        return jnp.where(ok, cand, thr), jnp.where(ok, cnt, cnt_t)

    thr, cnt_t = lax.fori_loop(0, 31, bit_body, (thr0, cnt_t0))
    has_tie = jnp.max(cnt_t) > kf
    bias_on, bias_off = 0.0, -jnp.inf

    @pl.when(jnp.logical_not(has_tie))
    def _():
        tmin = jnp.maximum(thr, INT_MIN + 1)

        def mask_chunk(c):
            off = pl.multiple_of(c * ck, ck)
            bias = jnp.where(key_ref[pl.ds(off, ck), :] >= tmin, bias_on, bias_off)
            key_ref[pl.ds(off, ck), :] = lax.bitcast_convert_type(bias.T, I32)
        for_chunks(mask_chunk)

    @pl.when(has_tie)
    def _():
        n_gt = count_where(lambda kc, base: kc > thr)
        need = kf - n_gt

        def pos_body(b, x):
            cand = x + jnp.left_shift(jnp.int32(1), b)
            cnt = count_where(lambda kc, base: (kc == thr) & (k_in + base < cand))
            return jnp.where(cnt < need, cand, x)

        nbits = int(math.ceil(math.log2(key_ref.shape[0]))) + 1
        x = lax.fori_loop(0, nbits, lambda b, x: pos_body(nbits - 1 - b, x), jnp.zeros((1, tq), I32))
        x = jnp.where(thr > INT_MIN, x, -1)

        def body(c, carry):
            off = pl.multiple_of(c * ck, ck)
            kc = key_ref[pl.ds(off, ck), :]
            sel = (kc > thr) | ((kc == thr) & (k_in + c * ck <= x))
            key_ref[pl.ds(off, ck), :] = lax.bitcast_convert_type(jnp.where(sel, bias_on, bias_off).T, I32)
            return carry
        lax.fori_loop(0, nk, body, 0)

    ncol = ck // LANES

    def lane_fold(x, op):
        out = x[:, 0:LANES]
        for r in range(1, ncol):
            out = op(out, x[:, r * LANES:(r + 1) * LANES])
        return out

    def scores(c, h):
        off = pl.multiple_of(c * ck, ck)
        hs = slice(h * hd, (h + 1) * hd)
        bias = lax.bitcast_convert_type(key_ref[pl.ds(off, ck), :], F32)
        return _dot_nt(qn_ref[:, hs], kb_ref[pl.ds(off, ck), hs]) + bias

    def sum_pass():
        acc_ref[...] = jnp.zeros(acc_ref.shape, F32)
        l_ref[...] = jnp.zeros(l_ref.shape, F32)

        def sum_chunk(c):
            off = pl.multiple_of(c * ck, ck)
            for h in range(A_HEADS):
                hs = slice(h * hd, (h + 1) * hd)
                p = jnp.exp2(scores(c, h) - jnp.tile(mx_ref[h], (1, ncol)))
                l_ref[h] += lane_fold(p, jnp.add)
                acc_ref[h] += _dot(p.astype(BF16), vb_ref[pl.ds(off, ck), hs])

        for_chunks(sum_chunk)
        return [jnp.sum(l_ref[h], axis=-1, keepdims=True) for h in range(A_HEADS)]

    for h in range(A_HEADS):
        qh = qn_ref[:, h * hd:(h + 1) * hd].astype(F32)
        bound = jnp.sqrt(jnp.sum(qh * qh, axis=-1, keepdims=True) * kmx_ref[h:h + 1, 0:1])
        mx_ref[h] = jnp.broadcast_to(bound, (tq, LANES))
    l_cols = sum_pass()
    l_min = l_cols[0]
    for h in range(1, A_HEADS):
        l_min = jnp.minimum(l_min, l_cols[h])

    def finish(l_cols):
        for h in range(A_HEADS):
            ha_ref[:, h * hd:(h + 1) * hd] = (acc_ref[h] / l_cols[h]).astype(BF16)

    finish(l_cols)

    @pl.when(jnp.logical_not(jnp.min(l_min) >= SOFTMAX_SUM_FLOOR))
    def _():
        mx_ref[...] = jnp.full(mx_ref.shape, -jnp.inf, F32)

        def max_body(c, carry):
            for h in range(A_HEADS):
                mx_ref[h] = jnp.maximum(mx_ref[h], lane_fold(scores(c, h), jnp.maximum))
            return carry

        lax.fori_loop(0, nk, max_body, 0)
        for h in range(A_HEADS):
            mx_ref[h] = jnp.broadcast_to(jnp.max(mx_ref[h], axis=-1, keepdims=True), (tq, LANES))
        finish(sum_pass())


def _dsa_prompt(p, ps, gq, kmx, ki, kb, vb, *, tq):
    t = p.shape[0]
    hd = gq.shape[1]
    topk = min(TOPK_MAX, t // 4)
    return pl.pallas_call(
        functools.partial(_dsa_prompt_kernel, tq=tq, topk=topk),
        out_shape=jax.ShapeDtypeStruct((t, GROUP), BF16),
        grid=(t // tq,),
        in_specs=[
            pl.BlockSpec((tq, GROUP), lambda i: (i, C_QA)),
            pl.BlockSpec((tq, GROUP), lambda i: (i, C_QI)),
            pl.BlockSpec((tq, LANES), lambda i: (i, 0)),
            pl.BlockSpec((1, hd), lambda i: (0, 0)),
            pl.BlockSpec((A_HEADS, LANES), lambda i: (0, 0)),
            _vmem_spec(), _vmem_spec(), _vmem_spec(),
        ],
        out_specs=pl.BlockSpec((tq, GROUP), lambda i: (i, 0)),
        scratch_shapes=[pltpu.VMEM((t, tq), I32),
                        pltpu.VMEM((IDX_HEADS * IDX_DIM, tq), BF16),
                        pltpu.VMEM((tq, GROUP), BF16),
                        pltpu.VMEM((LANES, tq), F32),
                        pltpu.VMEM((A_HEADS, tq, LANES), F32),
                        pltpu.VMEM((A_HEADS, tq, LANES), F32),
                        pltpu.VMEM((A_HEADS, tq, hd), F32)],
        compiler_params=_cparams(("arbitrary",)),
        name="dsa_prompt",
    )(p, p, ps, gq, kmx, ki, kb, vb)


def _dsa_sample_prep_kernel(q_ref, k_ref, v_ref, qi_ref, ps_ref, gq_ref, gk_ref,
                            qs_ref, kn_ref, vn_ref, sn_ref):
    hd = gq_ref.shape[1]
    qscale = (hd ** -0.5) * LOG2E
    for h in range(A_HEADS):
        hs = slice(h * hd, (h + 1) * hd)
        qs_ref[:, hs] = _rms(q_ref[:, hs], gq_ref[...]) * qscale
        kn_ref[:, hs] = _rms(k_ref[:, hs], gk_ref[...])
    vn_ref[...] = v_ref[...]
    ps = ps_ref[...]
    lane = lax.broadcasted_iota(I32, ps.shape, 1)
    ke = jnp.where(lane < IDX_DIM, ps, 0.0)
    ko = pltpu.roll(ke, IDX_DIM, axis=1)
    w_scale = (IDX_HEADS ** -0.5) * (IDX_DIM ** -0.5)
    acc = jnp.zeros((ps.shape[0], 1), F32)
    for pr in range(IDX_HEADS // 2):
        q2 = qi_ref[:, pr * LANES:(pr + 1) * LANES]
        se = jnp.sum(q2 * ke, axis=-1, keepdims=True)
        so = jnp.sum(q2 * ko, axis=-1, keepdims=True)
        acc = acc + jnp.maximum(se, 0.0) * (ps[:, S_WIDX + 2 * pr:S_WIDX + 2 * pr + 1] * w_scale)
        acc = acc + jnp.maximum(so, 0.0) * (ps[:, S_WIDX + 2 * pr + 1:S_WIDX + 2 * pr + 2] * w_scale)
    sn_ref[...] = jnp.broadcast_to(acc, sn_ref.shape)


def _dsa_sample_prep(p, ps, gq, gk):
    db = p.shape[0]
    hd = gq.shape[1]
    blk = lambda c: pl.BlockSpec((db, GROUP), lambda i: (0, c))
    full = lambda shape: pl.BlockSpec(shape, lambda i: (0,) * len(shape))
    return pl.pallas_call(
        _dsa_sample_prep_kernel,
        out_shape=(jax.ShapeDtypeStruct((db, GROUP), F32), jax.ShapeDtypeStruct((db, GROUP), F32),
                   jax.ShapeDtypeStruct((db, GROUP), F32), jax.ShapeDtypeStruct((db, LANES), F32)),
        grid=(1,),
        in_specs=[blk(C_QA), blk(C_KA), blk(C_VA), blk(C_QI), full((db, LANES)), full((1, hd)), full((1, hd))],
        out_specs=(full((db, GROUP)), full((db, GROUP)), full((db, GROUP)), full((db, LANES))),
        compiler_params=_cparams(("arbitrary",)),
        name="dsa_sample_prep",
    )(p, p, p, p, ps, gq, gk)


def _dsa_sample_scores_kernel(pt_ref, qi_ref, w_ref, ckt_ref, out_ref, pbuf, sem, *, n_pages, nseq):
    b = pl.program_id(0)
    slot = b % 2

    def page_copy(seq, p, sl):
        return pltpu.make_async_copy(ckt_ref.at[0, pt_ref[seq, p]], pbuf.at[sl, p], sem.at[sl])

    def issue(seq, sl):
        def body(p, carry):
            page_copy(seq, p, sl).start()
            return carry
        lax.fori_loop(0, n_pages, body, 0, unroll=8)

    @pl.when(b == 0)
    def _():
        issue(0, 0)

    @pl.when(b + 1 < nseq)
    def _():
        issue(b + 1, 1 - slot)

    def wait_body(p, carry):
        page_copy(b, p, slot).wait()
        return carry
    lax.fori_loop(0, n_pages, wait_body, 0, unroll=8)

    qi = qi_ref[0].astype(BF16)
    w = w_ref[0]
    two = 2 * PAGE_SIZE

    def score_body(p2, carry):
        kt = jnp.concatenate([pbuf[slot, 2 * p2], pbuf[slot, 2 * p2 + 1]], axis=1).astype(BF16)
        s = jnp.maximum(_dot(qi, kt), 0.0) * w
        out_ref[0, :, pl.ds(pl.multiple_of(p2 * two, two), two)] = jnp.sum(s, axis=0, keepdims=True)
        return carry
    lax.fori_loop(0, n_pages // 2, score_body, 0, unroll=8)


def _dsa_sample_scores(page_table, qi3, w3, cache_kidx_t):
    db, n_pages = page_table.shape
    assert n_pages % 16 == 0
    grid_spec = pltpu.PrefetchScalarGridSpec(
        num_scalar_prefetch=1,
        grid=(db,),
        in_specs=[pl.BlockSpec((1, IDX_HEADS, IDX_DIM), lambda b, pt: (b, 0, 0)),
                  pl.BlockSpec((1, IDX_HEADS, 1), lambda b, pt: (b, 0, 0)),
                  pl.BlockSpec(memory_space=pl.ANY)],
        out_specs=pl.BlockSpec((1, 1, n_pages * PAGE_SIZE), lambda b, pt: (b, 0, 0)),
        scratch_shapes=[pltpu.VMEM((2, n_pages, IDX_DIM, PAGE_SIZE), F32), pltpu.SemaphoreType.DMA((2,))],
    )
    return pl.pallas_call(
        functools.partial(_dsa_sample_scores_kernel, n_pages=n_pages, nseq=db),
        out_shape=jax.ShapeDtypeStruct((db, 1, n_pages * PAGE_SIZE), F32),
        grid_spec=grid_spec,
        compiler_params=_cparams(("arbitrary",)),
        name="dsa_sample_scores",
    )(page_table, qi3, w3, cache_kidx_t)


def _dsa_sample_select_kernel(sc_ref, sn_ref, ptt_ref, idx_ref, nv_ref, ns_ref, *, topk):
    db, npg, pgs = sc_ref.shape
    kf = float(topk)
    key = _mono_key(sc_ref[...])
    key_n = _mono_key(sn_ref[...][:, 0:1]).reshape(db, 1, 1)

    def total(x):
        return jnp.sum(jnp.sum(x, axis=2, keepdims=True), axis=1, keepdims=True)

    def count_ge(cand):
        return (total(jnp.where(key >= cand, 1.0, 0.0)) + jnp.where(key_n >= cand, 1.0, 0.0))

    cnt0 = count_ge(jnp.zeros((db, 1, 1), I32))
    thr0 = jnp.where(cnt0 >= kf, 0, INT_MIN).astype(I32)

    def bit_body(b, thr):
        cand = thr + jnp.left_shift(jnp.int32(1), 30 - b)
        return jnp.where(count_ge(cand) >= kf, cand, thr)

    thr = lax.fori_loop(0, 31, bit_body, thr0)
    gt = jnp.where(key > thr, 1.0, 0.0)
    eq = jnp.where(key == thr, 1.0, 0.0)
    gt_n = jnp.where(key_n > thr, 1.0, 0.0)
    eq_n = jnp.where(key_n == thr, 1.0, 0.0)
    need = kf - (total(gt) + gt_n)

    r_i = lax.broadcasted_iota(I32, (pgs, pgs), 0)
    c_i = lax.broadcasted_iota(I32, (pgs, pgs), 1)
    upper = jnp.where(r_i <= c_i, 1.0, 0.0).astype(BF16)
    lower = jnp.where(c_i <= r_i, 1.0, 0.0).astype(BF16)
    lower_pg = jnp.where(lax.broadcasted_iota(I32, (npg, npg), 1)
                         <= lax.broadcasted_iota(I32, (npg, npg), 0), 1.0, 0.0).astype(BF16)
    j_row = lax.broadcasted_iota(I32, (1, topk), 1).astype(F32)
    p_col = lax.broadcasted_iota(I32, (npg, 1), 0).astype(F32)

    def page_prefix(x):
        pin = _dot(x.astype(BF16), upper)
        rt = pin[:, pgs - 1:pgs]
        incl = _dot(lower_pg, jnp.broadcast_to(rt, (npg, pgs)).astype(BF16))[:, 0:1]
        return pin, rt, incl

    for b in range(db):
        eq_b = eq[b]
        pin_e, rt_e, incl_e = page_prefix(eq_b)
        rank_e = (incl_e - rt_e) + pin_e
        need_b = need[b]
        sel = gt[b] + jnp.where(rank_e <= need_b, eq_b, 0.0)
        n_eq = incl_e[npg - 1:npg, :]
        new_sel = gt_n[b] + jnp.where(n_eq + 1.0 <= need_b, eq_n[b], 0.0)

        pin, rt, incl = page_prefix(sel)
        ex = incl - rt
        n_sel = incl[npg - 1:npg, :]
        page_j = jnp.sum(jnp.where(incl <= j_row, 1.0, 0.0), axis=0, keepdims=True)
        oh_t = p_col == page_j
        ex_j = jnp.sum(jnp.where(oh_t, ex, 0.0), axis=0, keepdims=True)
        r_j = j_row - ex_j + 1.0
        pin_t = _dot(lower, sel.T.astype(BF16))
        rows_t = _dot(pin_t.astype(BF16), jnp.where(oh_t, 1.0, 0.0).astype(BF16))
        off_j = jnp.sum(jnp.where(rows_t < r_j, 1.0, 0.0), axis=0, keepdims=True)
        phys_j = jnp.sum(jnp.where(oh_t, ptt_ref[:, b:b + 1], 0.0), axis=0, keepdims=True)
        idx = jnp.where(j_row < n_sel, phys_j * float(pgs) + off_j, 0.0)
        idx_ref[b:b + 1, :] = idx.astype(I32)
        nv_ref[b:b + 1, :] = jnp.broadcast_to(n_sel, (1, LANES)).astype(I32)
        ns_ref[b:b + 1, :] = jnp.broadcast_to(new_sel, (1, LANES)).astype(I32)


def _dsa_sample_select(sc3, sn, pt_t, *, topk):
    db = sc3.shape[0]
    return pl.pallas_call(
        functools.partial(_dsa_sample_select_kernel, topk=topk),
        out_shape=(jax.ShapeDtypeStruct((db, topk), I32), jax.ShapeDtypeStruct((db, LANES), I32),
                   jax.ShapeDtypeStruct((db, LANES), I32)),
        in_specs=[_vmem_spec(), _vmem_spec(), _vmem_spec()],
        out_specs=(_vmem_spec(), _vmem_spec(), _vmem_spec()),
        compiler_params=_cparams(None),
        name="dsa_sample_select",
    )(sc3, sn, pt_t)


def _dsa_sample_attend_kernel(idx_ref, nv_ref, ns_ref, q_ref, kn_ref, vn_ref, ck_ref, cv_ref,
                              o_ref, kbuf, vbuf, sem, *, topk, nseq):
    b = pl.program_id(0)
    slot = b % 2
    nh = A_HEADS
    hd = q_ref.shape[2] // nh

    def row_copies(seq, j, sl):
        r = idx_ref[seq, j]
        dst = pl.ds(pl.multiple_of(j * nh, nh), nh)
        return (pltpu.make_async_copy(ck_ref.at[0, r], kbuf.at[sl, dst], sem.at[0, sl]),
                pltpu.make_async_copy(cv_ref.at[0, r], vbuf.at[sl, dst], sem.at[1, sl]))

    def issue(seq, sl):
        def body(j, carry):
            ck, cv = row_copies(seq, j, sl)
            ck.start()
            cv.start()
            return carry
        lax.fori_loop(0, topk, body, 0, unroll=8)

    @pl.when(b == 0)
    def _():
        issue(0, 0)

    @pl.when(b + 1 < nseq)
    def _():
        issue(b + 1, 1 - slot)

    def wait_body(j, carry):
        ck, cv = row_copies(b, j, slot)
        ck.wait()
        cv.wait()
        return carry
    lax.fori_loop(0, topk, wait_body, 0, unroll=8)

    n_valid = nv_ref[b]
    new_on = ns_ref[b] > 0
    j_row = lax.broadcasted_iota(I32, (1, topk), 1)
    for h in range(nh):
        hs = slice(h * hd, (h + 1) * hd)
        qh = q_ref[0, :, hs]
        kh = kbuf[slot, pl.ds(h, topk, stride=nh), :]
        vh = vbuf[slot, pl.ds(h, topk, stride=nh), :]
        s = _dot_nt(qh.astype(BF16), kh.astype(BF16))
        s = jnp.where(j_row < n_valid, s, -jnp.inf)
        s_new = jnp.sum(qh * kn_ref[0, :, hs], axis=-1, keepdims=True)
        s_new = jnp.where(new_on, s_new, -jnp.inf)
        m = jnp.maximum(jnp.max(s, axis=-1, keepdims=True), s_new)
        p = jnp.exp2(s - m)
        p_new = jnp.exp2(s_new - m)
        l = jnp.sum(p, axis=-1, keepdims=True) + p_new
        o = _dot(p.astype(BF16), vh.astype(BF16)) + p_new * vn_ref[0, :, hs]
        o_ref[0, :, hs] = o / l


def _dsa_sample_attend(idx, nv, ns, qs3, kn3, vn3, cache_k, cache_v, *, topk):
    db = idx.shape[0]
    w = qs3.shape[2]
    hd = w // A_HEADS
    row = pl.BlockSpec((1, 1, w), lambda b, *_: (b, 0, 0))
    grid_spec = pltpu.PrefetchScalarGridSpec(
        num_scalar_prefetch=3,
        grid=(db,),
        in_specs=[row, row, row, pl.BlockSpec(memory_space=pl.ANY), pl.BlockSpec(memory_space=pl.ANY)],
        out_specs=row,
        scratch_shapes=[pltpu.VMEM((2, topk * A_HEADS, hd), F32), pltpu.VMEM((2, topk * A_HEADS, hd), F32),
                        pltpu.SemaphoreType.DMA((2, 2))],
    )
    return pl.pallas_call(
        functools.partial(_dsa_sample_attend_kernel, topk=topk, nseq=db),
        out_shape=jax.ShapeDtypeStruct((db, 1, w), F32),
        grid_spec=grid_spec,
        compiler_params=_cparams(("arbitrary",)),
        name="dsa_sample_attend",
    )(idx, nv, ns, qs3, kn3, vn3, cache_k, cache_v)


def _merge_kernel(hm_ref, ha_ref, gm_ref, ga_ref, x_ref, wmo_ref, wao_ref, wout_ref, y_ref):
    t1 = _dot(hm_ref[...].astype(BF16), wmo_ref[...])
    t2 = _dot(ha_ref[...].astype(BF16), wao_ref[...])
    z = _sigmoid(gm_ref[...]) * t1 + _sigmoid(ga_ref[...]) * t2
    y_ref[...] = x_ref[...] + _dot(z.astype(BF16), wout_ref[...])


def _merge(hm, ha, p, x, wmo, wao, wout, *, tm):
    m, d = x.shape
    w = hm.shape[1]
    return pl.pallas_call(
        _merge_kernel,
        out_shape=jax.ShapeDtypeStruct((m, d), F32),
        grid=(pl.cdiv(m, tm),),
        in_specs=[
            pl.BlockSpec((tm, w), lambda i: (i, 0)),
            pl.BlockSpec((tm, w), lambda i: (i, 0)),
            pl.BlockSpec((tm, d), lambda i: (i, 0)),
            pl.BlockSpec((tm, d), lambda i: (i, 1)),
            pl.BlockSpec((tm, d), lambda i: (i, 0)),
            _vmem_spec(), _vmem_spec(), _vmem_spec(),
        ],
        out_specs=pl.BlockSpec((tm, d), lambda i: (i, 0)),
        compiler_params=_cparams(("arbitrary",)),
        name="merge",
    )(hm, ha, p, p, x, wmo, wao, wout)


FFN_TM = 512
FFN_TF = 512
PROJ_TM = 512
MERGE_TM = 256
MLSTM_ROWS = 256
MLSTM_SAMPLE_GROUP = 8
DSA_TQ = 256
CHUNKS_PER_TRIP = 4
DSA_PREP_TM = 512


def kernel(x_prompt, x_sample, cache_k, cache_v, cache_kidx, state_conv, state_C, state_n, state_m, page_table, norm_ffn1, w_ffn1_up, w_ffn1_down, norm_mix, w_in, b_if, w_conv, b_conv, w_qk_m, norm_mlstm_out, norm_q, norm_k, w_mo, w_ao, w_out, norm_ffn2, w_ffn2_up, w_ffn2_down):
    depth = w_in.shape[0]
    assert depth == 1, "single-layer step"
    bsz, seq, d = x_prompt.shape
    assert bsz == 1
    db, dseq, _ = x_sample.shape
    assert dseq == 1
    n_pages = page_table.shape[1]
    assert cache_k.shape[2] == PAGE_SIZE
    past = n_pages * PAGE_SIZE
    topk_s = min(TOPK_MAX, (past + dseq) // 4)
    l = 0

    xp = x_prompt.reshape(seq, d)
    xs = x_sample.reshape(db, d)

    wu1, wd1 = w_ffn1_up[l].astype(BF16), w_ffn1_down[l].astype(BF16)
    wu2, wd2 = w_ffn2_up[l].astype(BF16), w_ffn2_down[l].astype(BF16)
    w_t, w_small, tab = _prep_w_in(w_in[l], d)
    wmo, wao, wout = w_mo[l].astype(BF16), w_ao[l].astype(BF16), w_out[l].astype(BF16)
    wqk = w_qk_m[l].astype(BF16)
    g1, gmix, g2 = norm_ffn1[l][None], norm_mix[l][None], norm_ffn2[l][None]
    gq, gk = norm_q[l][None], norm_k[l][None]
    gmo = norm_mlstm_out[l]
    bif_row = jnp.zeros((1, LANES), F32).at[0, S_IGATE:S_IGATE + 2 * M_HEADS].set(b_if[l])
    wcv, bcv = w_conv[l], b_conv[l][None]

    xp1, hp = _ffn(xp, g1, wu1, wd1, tm=FFN_TM, tf=FFN_TF, g_next=gmix)
    xs1, hs = _ffn(xs, g1, wu1, wd1, tm=db, tf=FFN_TF, g_next=gmix)

    pp, pps = _inproj(hp, w_t, tab, tm=PROJ_TM), _inproj_small(hp, w_small, tm=PROJ_TM)
    sp, sps = _inproj(hs, w_t, tab, tm=db), _inproj_small(hs, w_small, tm=db)

    hm_p, conv_p, c_p, n_p, m_p = _mlstm_prompt(pp, pps, bif_row, wcv, bcv, wqk, gmo, rows=MLSTM_ROWS)
    k32, v32, kb, vb, ki, kmx = _dsa_prep(pp, pps, gk, tm=DSA_PREP_TM)
    ha_p = _dsa_prompt(pp, pps, gq, kmx, ki, kb, vb, tq=DSA_TQ)
    yp = _merge(hm_p, ha_p, pp, xp1, wmo, wao, wout, tm=MERGE_TM)
    out_p = _ffn(yp, g2, wu2, wd2, tm=FFN_TM, tf=FFN_TF)

    conv_t = jnp.transpose(state_conv[l], (1, 0, 2))
    hm_s, conv_s, c_s, n_s, m_s = _mlstm_sample(sp, sps, conv_t, state_C[l], state_n[l], state_m[l],
                                                bif_row, wcv, bcv, wqk, gmo, gb=MLSTM_SAMPLE_GROUP)
    qs, kn_s, vn_s, sn = _dsa_sample_prep(sp, sps, gq, gk)
    qi3 = sp[:, C_QI * GROUP:(C_QI + 1) * GROUP].reshape(db, IDX_HEADS, IDX_DIM)
    w3 = (sps[:, S_WIDX:S_WIDX + IDX_HEADS] * ((IDX_HEADS ** -0.5) * (IDX_DIM ** -0.5))).reshape(db, IDX_HEADS, 1)
    sc = _dsa_sample_scores(page_table, qi3, w3, jnp.swapaxes(cache_kidx, 2, 3))
    sc3 = sc.reshape(db, n_pages, PAGE_SIZE)
    idx, nv, ns = _dsa_sample_select(sc3, sn, page_table.T.astype(F32), topk=topk_s)
    n_rows = cache_k.shape[1] * PAGE_SIZE
    ha_s = _dsa_sample_attend(idx, nv[:, 0], ns[:, 0], qs.reshape(db, 1, GROUP),
                              kn_s.reshape(db, 1, GROUP), vn_s.reshape(db, 1, GROUP),
                              cache_k.reshape(depth, n_rows, A_HEADS, GROUP // A_HEADS),
                              cache_v.reshape(depth, n_rows, A_HEADS, GROUP // A_HEADS), topk=topk_s)
    ys = _merge(hm_s, ha_s.reshape(db, GROUP), sp, xs1, wmo, wao, wout, tm=db)
    out_s = _ffn(ys, g2, wu2, wd2, tm=db, tf=FFN_TF)

    hd = GROUP // A_HEADS
    hv = GROUP // M_HEADS
    hk = hv // 2
    return (
        out_p.reshape(bsz, seq, d),
        out_s.reshape(db, dseq, d),
        k32.reshape(depth, bsz, seq, A_HEADS, hd),
        v32.reshape(depth, bsz, seq, A_HEADS, hd),
        pps[:, S_KIDX:S_KIDX + IDX_DIM].reshape(depth, bsz, seq, IDX_DIM),
        conv_p.reshape(depth, bsz, CONV_W - 1, GROUP),
        c_p.reshape(depth, bsz, M_HEADS, hk, hv),
        n_p[:M_HEADS].reshape(depth, bsz, M_HEADS, hk),
        m_p[:M_HEADS, 0].reshape(depth, bsz, M_HEADS),
        kn_s.reshape(depth, db, dseq, A_HEADS, hd),
        vn_s.reshape(depth, db, dseq, A_HEADS, hd),
        sps[:, S_KIDX:S_KIDX + IDX_DIM].reshape(depth, db, dseq, IDX_DIM),
        jnp.transpose(conv_s, (1, 0, 2)).reshape(depth, db, CONV_W - 1, GROUP),
        c_s.reshape(depth, db, M_HEADS, hk, hv),
        n_s.reshape(depth, db, M_HEADS, hk),
        m_s.reshape(depth, db, M_HEADS),
    )
```

```python
import functools
import math

import jax
import jax.numpy as jnp
from jax import lax
from jax.experimental import pallas as pl
from jax.experimental.pallas import tpu as pltpu

F32 = jnp.float32
BF16 = jnp.bfloat16
I32 = jnp.int32

EPS = 1e-6
LANES = 128
PAGE_SIZE = 128
M_HEADS = 4
M_CHUNK = 64
CONV_W = 4
A_HEADS = 8
IDX_HEADS = 16
IDX_DIM = 64
TOPK_MAX = 256
N_BRANCHES = 2
GROUP = 1024
INT_MIN = -2 ** 31
NEG_INIT = -1e30
LOG2E = 1.4426950408889634
SOFTMAX_SUM_FLOOR = 2.0 ** -100
VMEM_LIMIT = 56 * 1024 * 1024

S_KIDX = 0
S_WIDX = IDX_DIM
S_IGATE = IDX_DIM + IDX_HEADS
S_FGATE = S_IGATE + M_HEADS

C_GATE, C_UM, C_VM, C_OM, C_QA, C_KA, C_VA, C_QI = 0, 4, 5, 6, 7, 8, 9, 10
N_BIG_BLOCKS = 11


def _cparams(sem, vmem=VMEM_LIMIT):
    return pltpu.CompilerParams(dimension_semantics=sem, vmem_limit_bytes=vmem)


def _vmem_spec():
    return pl.BlockSpec(memory_space=pltpu.VMEM)


def _rms(x, g):
    return x * lax.rsqrt(jnp.mean(x * x, axis=-1, keepdims=True) + EPS) * g


def _sigmoid(x):
    return 1.0 / (1.0 + jnp.exp(-x))


def _log_sigmoid(x):
    return jnp.minimum(x, 0.0) - jnp.log(1.0 + jnp.exp(-jnp.abs(x)))


def _dot(a, b):
    return jnp.dot(a, b, preferred_element_type=F32)


def _dot_nt(a, b):
    return lax.dot_general(a, b, (((1,), (1,)), ((), ())), preferred_element_type=F32)


def _dot_exact(a, b):
    return jnp.dot(a, b, preferred_element_type=F32, precision=lax.Precision.HIGHEST)


def _mono_key(x):
    b = lax.bitcast_convert_type(x, I32)
    return b ^ ((b >> 31) & jnp.int32(0x7FFFFFFF))


def _ffn_kernel(x_ref, g_ref, gn_ref, wa_ref, wb_ref, wd_ref, wat_ref, wbt_ref, wdt_ref, o_ref, *rest,
                nfull, emit_next):
    hn_ref, h_ref = rest if emit_next else (None, rest[0])
    j = pl.program_id(1)

    @pl.when(j == 0)
    def _():
        h_ref[...] = _rms(x_ref[...], g_ref[...]).astype(BF16)

    def half_step(wa, wb, wd):
        h = h_ref[...]
        a = _dot(h, wa[...])
        act = (a * _sigmoid(a) * _dot(h, wb[...])).astype(BF16)
        return _dot(act, wd[...])

    @pl.when(j == 0)
    def _():
        o_ref[...] = half_step(wa_ref, wb_ref, wd_ref)

    @pl.when((j > 0) & (j < nfull))
    def _():
        o_ref[...] += half_step(wa_ref, wb_ref, wd_ref)

    @pl.when(j == nfull)
    def _():
        y = x_ref[...] + 0.5 * (o_ref[...] + half_step(wat_ref, wbt_ref, wdt_ref))
        o_ref[...] = y
        if emit_next:
            hn_ref[...] = _rms(y, gn_ref[...]).astype(BF16)


def _ffn(x, g, w_up, w_down, *, tm, tf, g_next=None):
    emit_next = g_next is not None
    m, d = x.shape
    d_ff = w_down.shape[0]
    nfull = d_ff // tf
    tail = d_ff - nfull * tf
    assert tail > 0 and tail % LANES == 0 and d_ff % LANES == 0
    elem = lambda r, c: (pl.Element(r), pl.Element(c))
    main = lambda j: jnp.minimum(j, nfull - 1) * tf
    main_b = lambda j: (d_ff // LANES + jnp.minimum(j, nfull - 1) * (tf // LANES)) * LANES
    tail_b = (d_ff + nfull * tf) // LANES
    row_out = pl.BlockSpec((tm, d), lambda i, j: (i, 0))
    y_shape = jax.ShapeDtypeStruct((m, d), F32)
    return pl.pallas_call(
        functools.partial(_ffn_kernel, nfull=nfull, emit_next=emit_next),
        out_shape=(y_shape, jax.ShapeDtypeStruct((m, d), BF16)) if emit_next else y_shape,
        grid=(pl.cdiv(m, tm), nfull + 1),
        in_specs=[
            pl.BlockSpec((tm, d), lambda i, j: (i, 0)),
            pl.BlockSpec((1, d), lambda i, j: (0, 0)),
            pl.BlockSpec((1, d), lambda i, j: (0, 0)),
            pl.BlockSpec(elem(d, tf), lambda i, j: (0, main(j))),
            pl.BlockSpec(elem(d, tf), lambda i, j: (0, main_b(j))),
            pl.BlockSpec(elem(tf, d), lambda i, j: (main(j), 0)),
            pl.BlockSpec(elem(d, tail), lambda i, j: (0, nfull * tf)),
            pl.BlockSpec(elem(d, tail), lambda i, j: (0, tail_b * LANES)),
            pl.BlockSpec(elem(tail, d), lambda i, j: (nfull * tf, 0)),
        ],
        out_specs=(row_out, row_out) if emit_next else row_out,
        scratch_shapes=[pltpu.VMEM((tm, d), BF16)],
        compiler_params=_cparams(("arbitrary", "arbitrary")),
        name="ffn",
    )(x, g, g_next if emit_next else g, w_up, w_up, w_down, w_up, w_up, w_down)


SUBLANES = 8


def _inproj_kernel(tab_ref, h_ref, wt_ref, p_ref, w_ref):
    @pl.when(pl.program_id(1) == 0)
    def _():
        w_ref[...] = wt_ref[...].T.astype(BF16)

    p_ref[...] = _dot(h_ref[...], w_ref[...])


def _inproj(h, w_t, tab, *, tm):
    m, d = h.shape
    n_groups = tab.shape[0]
    grid_spec = pltpu.PrefetchScalarGridSpec(
        num_scalar_prefetch=1,
        grid=(n_groups, pl.cdiv(m, tm)),
        in_specs=[
            pl.BlockSpec((tm, d), lambda g, i, tab: (i, 0)),
            pl.BlockSpec((pl.Element(GROUP), pl.Element(d)), lambda g, i, tab: (tab[g] * SUBLANES, 0)),
        ],
        out_specs=pl.BlockSpec((tm, GROUP), lambda g, i, tab: (i, g)),
        scratch_shapes=[pltpu.VMEM((d, GROUP), BF16)],
    )
    return pl.pallas_call(
        _inproj_kernel,
        out_shape=jax.ShapeDtypeStruct((m, n_groups * GROUP), F32),
        grid_spec=grid_spec,
        compiler_params=_cparams(("arbitrary", "arbitrary")),
        name="inproj",
    )(tab, h, w_t)


def _inproj_small_kernel(h_ref, wst_ref, ps_ref):
    ps_ref[...] = _dot(h_ref[...], wst_ref[...].T.astype(BF16))


def _inproj_small(h, w_small_t, *, tm):
    m, d = h.shape
    return pl.pallas_call(
        _inproj_small_kernel,
        out_shape=jax.ShapeDtypeStruct((m, LANES), F32),
        grid=(pl.cdiv(m, tm),),
        in_specs=[pl.BlockSpec((tm, d), lambda i: (i, 0)), pl.BlockSpec((LANES, d), lambda i: (0, 0))],
        out_specs=pl.BlockSpec((tm, LANES), lambda i: (i, 0)),
        compiler_params=_cparams(("arbitrary",)),
        name="inproj_small",
    )(h, w_small_t)


def _prep_w_in(w_in, d_model):
    mw = GROUP
    sizes = (mw, mw, mw, M_HEADS, M_HEADS, GROUP, GROUP, GROUP, IDX_HEADS * IDX_DIM, IDX_DIM,
             IDX_HEADS, N_BRANCHES * d_model)
    offs = [0]
    for s in sizes:
        offs.append(offs[-1] + s)
    u_m, v_m, o_m, i_m, f_m, q_a, k_a, v_a, q_i, k_i, w_i, gate = range(12)
    starts = [offs[gate] + r * GROUP for r in range(N_BRANCHES * d_model // GROUP)]
    starts += [offs[k] for k in (u_m, v_m, o_m, q_a, k_a, v_a, q_i)]
    assert len(starts) == N_BIG_BLOCKS and all(s % SUBLANES == 0 for s in starts)
    tab = jnp.array([s // SUBLANES for s in starts], I32)
    w_t = jnp.swapaxes(w_in, 0, 1)
    rows = lambda k: w_t[offs[k]:offs[k + 1]]
    fill = jnp.zeros((LANES - S_FGATE - M_HEADS, w_in.shape[0]), w_in.dtype)
    w_small_t = jnp.concatenate([rows(k_i), rows(w_i), rows(i_m), rows(f_m), fill], axis=0)
    return w_t, w_small_t, tab


def _mlstm_prompt_kernel(u_ref, v_ref, o_ref, ps_ref, bif_ref, wconv_ref, bconv_ref, wqk_ref, gmo_ref,
                         hm_ref, conv_ref, cst_ref, nst_ref, mst_ref,
                         ext_ref, c_ref, n_ref, m_ref, *, rows, nsteps):
    i = pl.program_id(0)
    mw = u_ref.shape[1]
    hv = mw // M_HEADS
    hk = hv // 2
    L = M_CHUNK

    @pl.when(i == 0)
    def _():
        ext_ref[0:8, :] = jnp.zeros((8, mw), F32)
        c_ref[...] = jnp.zeros(c_ref.shape, F32)
        n_ref[...] = jnp.zeros(n_ref.shape, F32)
        m_ref[...] = jnp.zeros(m_ref.shape, F32)

    u = u_ref[...]
    ext_ref[8:8 + rows, :] = u
    wc = wconv_ref[...]
    cv = (ext_ref[5:5 + rows, :] * wc[0:1] + ext_ref[6:6 + rows, :] * wc[1:2]
          + ext_ref[7:7 + rows, :] * wc[2:3] + u * wc[3:4]) + bconv_ref[...]
    ext_ref[0:8, :] = u[rows - 8:rows, :]
    cb = (cv * _sigmoid(cv)).astype(BF16)

    g = ps_ref[...] + bif_ref[...]
    lf = _log_sigmoid(g)
    r_i = lax.broadcasted_iota(I32, (rows, rows), 0)
    c_i = lax.broadcasted_iota(I32, (rows, rows), 1)
    tri = jnp.where((r_i // L == c_i // L) & (c_i <= r_i), 1.0, 0.0).astype(F32)
    bcum = _dot_exact(tri, lf)
    g_t = g.T
    b_t = bcum.T
    causal = lax.broadcasted_iota(I32, (L, L), 1) <= lax.broadcasted_iota(I32, (L, L), 0)

    for h in range(M_HEADS):
        hs = slice(h * hv, (h + 1) * hv)
        qk = _dot(cb[:, hs], wqk_ref[h])
        q = qk[:, :hk] * (hk ** -0.5)
        k = qk[:, hk:]
        v = v_ref[:, hs]
        li_col = g[:, S_IGATE + h:S_IGATE + h + 1]
        b_col = bcum[:, S_FGATE + h:S_FGATE + h + 1]
        li_row = g_t[S_IGATE + h:S_IGATE + h + 1, :]
        b_row = b_t[S_FGATE + h:S_FGATE + h + 1, :]
        cmat = c_ref[h]
        nvec = n_ref[h:h + 1, :]
        m = m_ref[h:h + 1, 0:1]
        for c in range(rows // L):
            sl = slice(c * L, (c + 1) * L)
            qc, kc, vc = q[sl], k[sl], v[sl]
            bc, lic = b_col[sl], li_col[sl]
            br, lir = b_row[:, sl], li_row[:, sl]
            qcb, kcb, vcb = qc.astype(BF16), kc.astype(BF16), vc.astype(BF16)
            dmat = jnp.where(causal, (bc - br) + lir, -jnp.inf)
            inter = bc + m
            m_t = jnp.maximum(inter, jnp.max(dmat, axis=-1, keepdims=True))
            s_inter = jnp.exp(inter - m_t)
            s = _dot_nt(qcb, kcb) * jnp.exp(dmat - m_t)
            num = _dot(s.astype(BF16), vcb) + s_inter * _dot(qcb, cmat.astype(BF16))
            den = (jnp.sum(s, axis=-1, keepdims=True)
                   + s_inter * jnp.sum(qc * nvec, axis=-1, keepdims=True))
            hc = num / jnp.maximum(jnp.abs(den), jnp.exp(-m_t))
            hn = _rms(hc, gmo_ref[h:h + 1, :]) * _sigmoid(o_ref[sl, hs])
            hm_ref[sl, hs] = hn.astype(BF16)
            b_last = bc[L - 1:L, :]
            g_row = (b_last - br) + lir
            g_col = (b_last - bc) + lic
            m_new = jnp.maximum(b_last + m, jnp.max(g_row, axis=-1, keepdims=True))
            s_old = jnp.exp(b_last + m - m_new)
            kw = kc * jnp.exp(g_col - m_new)
            cmat = s_old * cmat + _dot(kw.T.astype(BF16), vcb)
            nvec = s_old * nvec + jnp.sum(kw, axis=0, keepdims=True)
            m = m_new
        c_ref[h] = cmat
        n_ref[h:h + 1, :] = nvec
        m_ref[h:h + 1, :] = jnp.broadcast_to(m, (1, LANES))

    @pl.when(i == nsteps - 1)
    def _():
        conv_ref[...] = ext_ref[8 + rows - (CONV_W - 1):8 + rows, :]
        cst_ref[...] = c_ref[...]
        nst_ref[...] = n_ref[...]
        mst_ref[...] = m_ref[...]


def _mlstm_prompt(p, ps, bif_row, w_conv, b_conv, wqk, gmo, *, rows):
    t = p.shape[0]
    mw = GROUP
    hv = mw // M_HEADS
    hk = hv // 2
    nsteps = t // rows
    const = lambda shape: pl.BlockSpec(shape, lambda i: (0,) * len(shape))
    return pl.pallas_call(
        functools.partial(_mlstm_prompt_kernel, rows=rows, nsteps=nsteps),
        out_shape=(jax.ShapeDtypeStruct((t, mw), BF16),
                   jax.ShapeDtypeStruct((CONV_W - 1, mw), F32),
                   jax.ShapeDtypeStruct((M_HEADS, hk, hv), F32),
                   jax.ShapeDtypeStruct((8, hk), F32),
                   jax.ShapeDtypeStruct((8, LANES), F32)),
        grid=(nsteps,),
        in_specs=[
            pl.BlockSpec((rows, mw), lambda i: (i, C_UM)),
            pl.BlockSpec((rows, mw), lambda i: (i, C_VM)),
            pl.BlockSpec((rows, mw), lambda i: (i, C_OM)),
            pl.BlockSpec((rows, LANES), lambda i: (i, 0)),
            const((1, LANES)), const((CONV_W, mw)), const((1, mw)),
            const((M_HEADS, hv, 2 * hk)), const((M_HEADS, hv)),
        ],
        out_specs=(pl.BlockSpec((rows, mw), lambda i: (i, 0)),
                   const((CONV_W - 1, mw)), const((M_HEADS, hk, hv)), const((8, hk)), const((8, LANES))),
        scratch_shapes=[pltpu.VMEM((rows + 8, mw), F32), pltpu.VMEM((M_HEADS, hk, hv), F32),
                        pltpu.VMEM((8, hk), F32), pltpu.VMEM((8, LANES), F32)],
        compiler_params=_cparams(("arbitrary",)),
        name="mlstm_prompt",
    )(p, p, p, ps, bif_row, w_conv, b_conv, wqk, gmo)


def _mlstm_sample_kernel(u_ref, v_ref, o_ref, ps_ref, sc_ref, c0_ref, n0_ref, m0_ref,
                         bif_ref, wconv_ref, bconv_ref, wqk_ref, gmo_ref,
                         hm_ref, conv_ref, c1_ref, n1_ref, m1_ref, *, gb):
    mw = u_ref.shape[1]
    hv = mw // M_HEADS
    hk = hv // 2
    u = u_ref[...]
    wc = wconv_ref[...]
    cv = (sc_ref[0] * wc[0:1] + sc_ref[1] * wc[1:2] + sc_ref[2] * wc[2:3] + u * wc[3:4]) + bconv_ref[...]
    conv_ref[0] = sc_ref[1]
    conv_ref[1] = sc_ref[2]
    conv_ref[2] = u
    cb = (cv * _sigmoid(cv)).astype(BF16)
    g = ps_ref[...] + bif_ref[...]
    lf = _log_sigmoid(g)
    for h in range(M_HEADS):
        hs = slice(h * hv, (h + 1) * hv)
        qk = _dot(cb[:, hs], wqk_ref[h])
        q = qk[:, :hk] * (hk ** -0.5)
        k = qk[:, hk:]
        fill = jnp.zeros((LANES - gb, hk), F32)
        q_t = jnp.concatenate([q, fill], axis=0).T
        k_t = jnp.concatenate([k, fill], axis=0).T
        v = v_ref[:, hs]
        for s in range(gb):
            li = g[s:s + 1, S_IGATE + h:S_IGATE + h + 1]
            b = lf[s:s + 1, S_FGATE + h:S_FGATE + h + 1]
            m0 = m0_ref[s:s + 1, h:h + 1]
            qr, kr, vr = q[s:s + 1], k[s:s + 1], v[s:s + 1]
            cmat = c0_ref[s, h]
            nrow = n0_ref[s, h:h + 1, :]
            inter = b + m0
            m_t = jnp.maximum(inter, li)
            s_inter = jnp.exp(inter - m_t)
            dw = jnp.exp(li - m_t)
            sqk = jnp.sum(qr * kr, axis=-1, keepdims=True) * dw
            qc = jnp.sum(q_t[:, s:s + 1] * cmat, axis=0, keepdims=True)
            num = sqk * vr + s_inter * qc
            den = sqk + s_inter * jnp.sum(qr * nrow, axis=-1, keepdims=True)
            hc = num / jnp.maximum(jnp.abs(den), jnp.exp(-m_t))
            hn = _rms(hc, gmo_ref[h:h + 1, :]) * _sigmoid(o_ref[s:s + 1, hs])
            hm_ref[s:s + 1, hs] = hn.astype(BF16)
            c1_ref[s, h] = s_inter * cmat + (dw * k_t[:, s:s + 1]) * vr
            n1_ref[s, h:h + 1, :] = s_inter * nrow + dw * kr
            m1_ref[s:s + 1, h:h + 1] = m_t


def _mlstm_sample(p, ps, conv_t, c0, n0, m0, bif_row, w_conv, b_conv, wqk, gmo, *, gb):
    db = p.shape[0]
    mw = GROUP
    hv = mw // M_HEADS
    hk = hv // 2
    const = lambda shape: pl.BlockSpec(shape, lambda i: (0,) * len(shape))
    return pl.pallas_call(
        functools.partial(_mlstm_sample_kernel, gb=gb),
        out_shape=(jax.ShapeDtypeStruct((db, mw), BF16),
                   jax.ShapeDtypeStruct((CONV_W - 1, db, mw), F32),
                   jax.ShapeDtypeStruct((db, M_HEADS, hk, hv), F32),
                   jax.ShapeDtypeStruct((db, M_HEADS, hk), F32),
                   jax.ShapeDtypeStruct((db, M_HEADS), F32)),
        grid=(db // gb,),
        in_specs=[
            pl.BlockSpec((gb, mw), lambda i: (i, C_UM)),
            pl.BlockSpec((gb, mw), lambda i: (i, C_VM)),
            pl.BlockSpec((gb, mw), lambda i: (i, C_OM)),
            pl.BlockSpec((gb, LANES), lambda i: (i, 0)),
            pl.BlockSpec((CONV_W - 1, gb, mw), lambda i: (0, i, 0)),
            pl.BlockSpec((gb, M_HEADS, hk, hv), lambda i: (i, 0, 0, 0)),
            pl.BlockSpec((gb, M_HEADS, hk), lambda i: (i, 0, 0)),
            pl.BlockSpec((gb, M_HEADS), lambda i: (i, 0)),
            const((1, LANES)), const((CONV_W, mw)), const((1, mw)),
            const((M_HEADS, hv, 2 * hk)), const((M_HEADS, hv)),
        ],
        out_specs=(pl.BlockSpec((gb, mw), lambda i: (i, 0)),
                   pl.BlockSpec((CONV_W - 1, gb, mw), lambda i: (0, i, 0)),
                   pl.BlockSpec((gb, M_HEADS, hk, hv), lambda i: (i, 0, 0, 0)),
                   pl.BlockSpec((gb, M_HEADS, hk), lambda i: (i, 0, 0)),
                   pl.BlockSpec((gb, M_HEADS), lambda i: (i, 0))),
        compiler_params=_cparams(("arbitrary",)),
        name="mlstm_sample",
    )(p, p, p, ps, conv_t, c0, n0, m0, bif_row, w_conv, b_conv, wqk, gmo)


def _dsa_prep_kernel(k_ref, v_ref, ps_ref, gk_ref, k32_ref, v32_ref, kb_ref, vb_ref, ki_ref, kmx_ref):
    hd = gk_ref.shape[1]

    @pl.when(pl.program_id(0) == 0)
    def _():
        kmx_ref[...] = jnp.zeros(kmx_ref.shape, F32)

    for h in range(A_HEADS):
        hs = slice(h * hd, (h + 1) * hd)
        kn = _rms(k_ref[:, hs], gk_ref[...])
        k32_ref[:, hs] = kn
        kb = kn.astype(BF16)
        kb_ref[:, hs] = kb
        n2 = jnp.max(jnp.sum(jnp.square(kb.astype(F32)), axis=-1, keepdims=True), axis=0, keepdims=True)
        kmx_ref[h:h + 1, :] = jnp.maximum(kmx_ref[h:h + 1, :], n2)
    v = v_ref[...]
    v32_ref[...] = v
    vb_ref[...] = v.astype(BF16)
    ki_ref[...] = ps_ref[...].astype(BF16)


def _dsa_prep(p, ps, gk, *, tm):
    t = p.shape[0]
    hd = gk.shape[1]
    row = lambda w, c: pl.BlockSpec((tm, w), lambda i: (i, c))
    return pl.pallas_call(
        _dsa_prep_kernel,
        out_shape=(jax.ShapeDtypeStruct((t, GROUP), F32), jax.ShapeDtypeStruct((t, GROUP), F32),
                   jax.ShapeDtypeStruct((t, GROUP), BF16), jax.ShapeDtypeStruct((t, GROUP), BF16),
                   jax.ShapeDtypeStruct((t, LANES), BF16), jax.ShapeDtypeStruct((A_HEADS, LANES), F32)),
        grid=(t // tm,),
        in_specs=[row(GROUP, C_KA), row(GROUP, C_VA), row(LANES, 0),
                  pl.BlockSpec((1, hd), lambda i: (0, 0))],
        out_specs=(row(GROUP, 0), row(GROUP, 0), row(GROUP, 0), row(GROUP, 0), row(LANES, 0),
                   pl.BlockSpec((A_HEADS, LANES), lambda i: (0, 0))),
        compiler_params=_cparams(("arbitrary",)),
        name="dsa_prep",
    )(p, p, ps, gk)


def _dsa_prompt_kernel(q_ref, qi_ref, ps_ref, gq_ref, kmx_ref, ki_ref, kb_ref, vb_ref,
                       ha_ref, key_ref, qit_ref, qn_ref, wt_ref, mx_ref, l_ref, acc_ref, *, tq, topk):
    i = pl.program_id(0)
    hd = gq_ref.shape[1]
    ck = tq
    nk = i + 1
    kf = float(topk)
    SUB = 8

    def fold(x, op):
        out = x[0:SUB]
        for r in range(1, ck // SUB):
            out = op(out, x[r * SUB:(r + 1) * SUB])
        return out

    qit_ref[...] = qi_ref[...].T.astype(BF16)
    wt_ref[...] = ps_ref[...].T * ((IDX_HEADS ** -0.5) * (IDX_DIM ** -0.5))
    qscale = (hd ** -0.5) * LOG2E
    for h in range(A_HEADS):
        hs = slice(h * hd, (h + 1) * hd)
        qn_ref[:, hs] = (_rms(q_ref[:, hs], gq_ref[...]) * qscale).astype(BF16)

    q_pos = i * tq + lax.broadcasted_iota(I32, (ck, tq), 1)
    k_in = lax.broadcasted_iota(I32, (ck, tq), 0)

    def for_chunks(body, width=CHUNKS_PER_TRIP):
        def trip(t, carry):
            for r in range(width):
                body(width * t + r)
            return carry
        lax.fori_loop(0, nk // width, trip, 0)

        def single(c, carry):
            body(c)
            return carry
        lax.fori_loop(nk - nk % width, nk, single, 0)

    def score_chunk(c):
        off = pl.multiple_of(c * ck, ck)
        ki = ki_ref[pl.ds(off, ck), 0:IDX_DIM]
        acc = jnp.zeros((ck, tq), F32)
        for h in range(IDX_HEADS):
            s = _dot(ki, qit_ref[h * IDX_DIM:(h + 1) * IDX_DIM, :])
            acc = acc + jnp.maximum(s, 0.0) * wt_ref[S_WIDX + h:S_WIDX + h + 1, :]
        key_ref[pl.ds(off, ck), :] = jnp.where(k_in + c * ck <= q_pos, _mono_key(acc), INT_MIN)

    for_chunks(score_chunk)

    def count_where(pred):
        def one(c, acc):
            off = pl.multiple_of(c * ck, ck)
            return acc + fold(jnp.where(pred(key_ref[pl.ds(off, ck), :], c * ck), 1.0, 0.0), jnp.add)
        acc = lax.fori_loop(0, nk // 2, lambda c2, a: one(2 * c2 + 1, one(2 * c2, a)), jnp.zeros((SUB, tq), F32))
        acc = lax.cond(nk % 2 == 1, lambda a: one(nk - 1, a), lambda a: a, acc)
        return jnp.sum(acc, axis=0, keepdims=True)

    def count_ge(cand):
        return count_where(lambda kc, base: kc >= cand)

    cnt0 = count_ge(jnp.zeros((1, tq), I32))
    ok0 = cnt0 >= kf
    thr0 = jnp.where(ok0, 0, INT_MIN).astype(I32)
    cnt_t0 = jnp.where(ok0, cnt0, kf)

    def bit_body(b, carry):
        thr, cnt_t = carry
        cand = thr + jnp.left_shift(jnp.int32(1), 30 - b)
        cnt = count_ge(cand)
        ok = cnt >= kf
        return jnp.where(ok, cand, thr), jnp.where(ok, cnt, cnt_t)

    thr, cnt_t = lax.fori_loop(0, 31, bit_body, (thr0, cnt_t0))
    has_tie = jnp.max(cnt_t) > kf
    bias_on, bias_off = 0.0, -jnp.inf

    @pl.when(jnp.logical_not(has_tie))
    def _():
        tmin = jnp.maximum(thr, INT_MIN + 1)

        def mask_chunk(c):
            off = pl.multiple_of(c * ck, ck)
            bias = jnp.where(key_ref[pl.ds(off, ck), :] >= tmin, bias_on, bias_off)
            key_ref[pl.ds(off, ck), :] = lax.bitcast_convert_type(bias.T, I32)
        for_chunks(mask_chunk)

    @pl.when(has_tie)
    def _():
        n_gt = count_where(lambda kc, base: kc > thr)
        need = kf - n_gt

        def pos_body(b, x):
            cand = x + jnp.left_shift(jnp.int32(1), b)
            cnt = count_where(lambda kc, base: (kc == thr) & (k_in + base < cand))
            return jnp.where(cnt < need, cand, x)

        nbits = int(math.ceil(math.log2(key_ref.shape[0]))) + 1
        x = lax.fori_loop(0, nbits, lambda b, x: pos_body(nbits - 1 - b, x), jnp.zeros((1, tq), I32))
        x = jnp.where(thr > INT_MIN, x, -1)

        def body(c, carry):
            off = pl.multiple_of(c * ck, ck)
            kc = key_ref[pl.ds(off, ck), :]
            sel = (kc > thr) | ((kc == thr) & (k_in + c * ck <= x))
            key_ref[pl.ds(off, ck), :] = lax.bitcast_convert_type(jnp.where(sel, bias_on, bias_off).T, I32)
            return carry
        lax.fori_loop(0, nk, body, 0)

    ncol = ck // LANES

    def lane_fold(x, op):
        out = x[:, 0:LANES]
        for r in range(1, ncol):
            out = op(out, x[:, r * LANES:(r + 1) * LANES])
        return out

    def scores(c, h):
        off = pl.multiple_of(c * ck, ck)
        hs = slice(h * hd, (h + 1) * hd)
        bias = lax.bitcast_convert_type(key_ref[pl.ds(off, ck), :], F32)
        return _dot_nt(qn_ref[:, hs], kb_ref[pl.ds(off, ck), hs]) + bias

    def sum_pass():
        acc_ref[...] = jnp.zeros(acc_ref.shape, F32)
        l_ref[...] = jnp.zeros(l_ref.shape, F32)

        def sum_chunk(c):
            off = pl.multiple_of(c * ck, ck)
            for h in range(A_HEADS):
                hs = slice(h * hd, (h + 1) * hd)
                p = jnp.exp2(scores(c, h) - jnp.tile(mx_ref[h], (1, ncol)))
                l_ref[h] += lane_fold(p, jnp.add)
                acc_ref[h] += _dot(p.astype(BF16), vb_ref[pl.ds(off, ck), hs])

        for_chunks(sum_chunk)
        return [jnp.sum(l_ref[h], axis=-1, keepdims=True) for h in range(A_HEADS)]

    for h in range(A_HEADS):
        qh = qn_ref[:, h * hd:(h + 1) * hd].astype(F32)
        bound = jnp.sqrt(jnp.sum(qh * qh, axis=-1, keepdims=True) * kmx_ref[h:h + 1, 0:1])
        mx_ref[h] = jnp.broadcast_to(bound, (tq, LANES))
    l_cols = sum_pass()
    l_min = l_cols[0]
    for h in range(1, A_HEADS):
        l_min = jnp.minimum(l_min, l_cols[h])

    def finish(l_cols):
        for h in range(A_HEADS):
            ha_ref[:, h * hd:(h + 1) * hd] = (acc_ref[h] / l_cols[h]).astype(BF16)

    finish(l_cols)

    @pl.when(jnp.logical_not(jnp.min(l_min) >= SOFTMAX_SUM_FLOOR))
    def _():
        mx_ref[...] = jnp.full(mx_ref.shape, -jnp.inf, F32)

        def max_body(c, carry):
            for h in range(A_HEADS):
                mx_ref[h] = jnp.maximum(mx_ref[h], lane_fold(scores(c, h), jnp.maximum))
            return carry

        lax.fori_loop(0, nk, max_body, 0)
        for h in range(A_HEADS):
            mx_ref[h] = jnp.broadcast_to(jnp.max(mx_ref[h], axis=-1, keepdims=True), (tq, LANES))
        finish(sum_pass())


def _dsa_prompt(p, ps, gq, kmx, ki, kb, vb, *, tq):
    t = p.shape[0]
    hd = gq.shape[1]
    topk = min(TOPK_MAX, t // 4)
    return pl.pallas_call(
        functools.partial(_dsa_prompt_kernel, tq=tq, topk=topk),
        out_shape=jax.ShapeDtypeStruct((t, GROUP), BF16),
        grid=(t // tq,),
        in_specs=[
            pl.BlockSpec((tq, GROUP), lambda i: (i, C_QA)),
            pl.BlockSpec((tq, GROUP), lambda i: (i, C_QI)),
            pl.BlockSpec((tq, LANES), lambda i: (i, 0)),
            pl.BlockSpec((1, hd), lambda i: (0, 0)),
            pl.BlockSpec((A_HEADS, LANES), lambda i: (0, 0)),
            _vmem_spec(), _vmem_spec(), _vmem_spec(),
        ],
        out_specs=pl.BlockSpec((tq, GROUP), lambda i: (i, 0)),
        scratch_shapes=[pltpu.VMEM((t, tq), I32),
                        pltpu.VMEM((IDX_HEADS * IDX_DIM, tq), BF16),
                        pltpu.VMEM((tq, GROUP), BF16),
                        pltpu.VMEM((LANES, tq), F32),
                        pltpu.VMEM((A_HEADS, tq, LANES), F32),
                        pltpu.VMEM((A_HEADS, tq, LANES), F32),
                        pltpu.VMEM((A_HEADS, tq, hd), F32)],
        compiler_params=_cparams(("arbitrary",)),
        name="dsa_prompt",
    )(p, p, ps, gq, kmx, ki, kb, vb)


def _dsa_sample_prep_kernel(q_ref, k_ref, v_ref, qi_ref, ps_ref, gq_ref, gk_ref,
                            qs_ref, kn_ref, vn_ref, sn_ref):
    hd = gq_ref.shape[1]
    qscale = (hd ** -0.5) * LOG2E
    for h in range(A_HEADS):
        hs = slice(h * hd, (h + 1) * hd)
        qs_ref[:, hs] = _rms(q_ref[:, hs], gq_ref[...]) * qscale
        kn_ref[:, hs] = _rms(k_ref[:, hs], gk_ref[...])
    vn_ref[...] = v_ref[...]
    ps = ps_ref[...]
    lane = lax.broadcasted_iota(I32, ps.shape, 1)
    ke = jnp.where(lane < IDX_DIM, ps, 0.0)
    ko = pltpu.roll(ke, IDX_DIM, axis=1)
    w_scale = (IDX_HEADS ** -0.5) * (IDX_DIM ** -0.5)
    acc = jnp.zeros((ps.shape[0], 1), F32)
    for pr in range(IDX_HEADS // 2):
        q2 = qi_ref[:, pr * LANES:(pr + 1) * LANES]
        se = jnp.sum(q2 * ke, axis=-1, keepdims=True)
        so = jnp.sum(q2 * ko, axis=-1, keepdims=True)
        acc = acc + jnp.maximum(se, 0.0) * (ps[:, S_WIDX + 2 * pr:S_WIDX + 2 * pr + 1] * w_scale)
        acc = acc + jnp.maximum(so, 0.0) * (ps[:, S_WIDX + 2 * pr + 1:S_WIDX + 2 * pr + 2] * w_scale)
    sn_ref[...] = jnp.broadcast_to(acc, sn_ref.shape)


def _dsa_sample_prep(p, ps, gq, gk):
    db = p.shape[0]
    hd = gq.shape[1]
    blk = lambda c: pl.BlockSpec((db, GROUP), lambda i: (0, c))
    full = lambda shape: pl.BlockSpec(shape, lambda i: (0,) * len(shape))
    return pl.pallas_call(
        _dsa_sample_prep_kernel,
        out_shape=(jax.ShapeDtypeStruct((db, GROUP), F32), jax.ShapeDtypeStruct((db, GROUP), F32),
                   jax.ShapeDtypeStruct((db, GROUP), F32), jax.ShapeDtypeStruct((db, LANES), F32)),
        grid=(1,),
        in_specs=[blk(C_QA), blk(C_KA), blk(C_VA), blk(C_QI), full((db, LANES)), full((1, hd)), full((1, hd))],
        out_specs=(full((db, GROUP)), full((db, GROUP)), full((db, GROUP)), full((db, LANES))),
        compiler_params=_cparams(("arbitrary",)),
        name="dsa_sample_prep",
    )(p, p, p, p, ps, gq, gk)


def _dsa_sample_scores_kernel(pt_ref, qi_ref, w_ref, ckt_ref, out_ref, pbuf, sem, *, n_pages, nseq):
    b = pl.program_id(0)
    slot = b % 2

    def page_copy(seq, p, sl):
        return pltpu.make_async_copy(ckt_ref.at[0, pt_ref[seq, p]], pbuf.at[sl, p], sem.at[sl])

    def issue(seq, sl):
        def body(p, carry):
            page_copy(seq, p, sl).start()
            return carry
        lax.fori_loop(0, n_pages, body, 0, unroll=8)

    @pl.when(b == 0)
    def _():
        issue(0, 0)

    @pl.when(b + 1 < nseq)
    def _():
        issue(b + 1, 1 - slot)

    def wait_body(p, carry):
        page_copy(b, p, slot).wait()
        return carry
    lax.fori_loop(0, n_pages, wait_body, 0, unroll=8)

    qi = qi_ref[0].astype(BF16)
    w = w_ref[0]
    two = 2 * PAGE_SIZE

    def score_body(p2, carry):
        kt = jnp.concatenate([pbuf[slot, 2 * p2], pbuf[slot, 2 * p2 + 1]], axis=1).astype(BF16)
        s = jnp.maximum(_dot(qi, kt), 0.0) * w
        out_ref[0, :, pl.ds(pl.multiple_of(p2 * two, two), two)] = jnp.sum(s, axis=0, keepdims=True)
        return carry
    lax.fori_loop(0, n_pages // 2, score_body, 0, unroll=8)


def _dsa_sample_scores(page_table, qi3, w3, cache_kidx_t):
    db, n_pages = page_table.shape
    assert n_pages % 16 == 0
    grid_spec = pltpu.PrefetchScalarGridSpec(
        num_scalar_prefetch=1,
        grid=(db,),
        in_specs=[pl.BlockSpec((1, IDX_HEADS, IDX_DIM), lambda b, pt: (b, 0, 0)),
                  pl.BlockSpec((1, IDX_HEADS, 1), lambda b, pt: (b, 0, 0)),
                  pl.BlockSpec(memory_space=pl.ANY)],
        out_specs=pl.BlockSpec((1, 1, n_pages * PAGE_SIZE), lambda b, pt: (b, 0, 0)),
        scratch_shapes=[pltpu.VMEM((2, n_pages, IDX_DIM, PAGE_SIZE), F32), pltpu.SemaphoreType.DMA((2,))],
    )
    return pl.pallas_call(
        functools.partial(_dsa_sample_scores_kernel, n_pages=n_pages, nseq=db),
        out_shape=jax.ShapeDtypeStruct((db, 1, n_pages * PAGE_SIZE), F32),
        grid_spec=grid_spec,
        compiler_params=_cparams(("arbitrary",)),
        name="dsa_sample_scores",
    )(page_table, qi3, w3, cache_kidx_t)


def _dsa_sample_select_kernel(sc_ref, sn_ref, ptt_ref, idx_ref, nv_ref, ns_ref, *, topk):
    db, npg, pgs = sc_ref.shape
    kf = float(topk)
    key = _mono_key(sc_ref[...])
    key_n = _mono_key(sn_ref[...][:, 0:1]).reshape(db, 1, 1)

    def total(x):
        return jnp.sum(jnp.sum(x, axis=2, keepdims=True), axis=1, keepdims=True)

    def count_ge(cand):
        return (total(jnp.where(key >= cand, 1.0, 0.0)) + jnp.where(key_n >= cand, 1.0, 0.0))

    cnt0 = count_ge(jnp.zeros((db, 1, 1), I32))
    thr0 = jnp.where(cnt0 >= kf, 0, INT_MIN).astype(I32)

    def bit_body(b, thr):
        cand = thr + jnp.left_shift(jnp.int32(1), 30 - b)
        return jnp.where(count_ge(cand) >= kf, cand, thr)

    thr = lax.fori_loop(0, 31, bit_body, thr0)
    gt = jnp.where(key > thr, 1.0, 0.0)
    eq = jnp.where(key == thr, 1.0, 0.0)
    gt_n = jnp.where(key_n > thr, 1.0, 0.0)
    eq_n = jnp.where(key_n == thr, 1.0, 0.0)
    need = kf - (total(gt) + gt_n)

    r_i = lax.broadcasted_iota(I32, (pgs, pgs), 0)
    c_i = lax.broadcasted_iota(I32, (pgs, pgs), 1)
    upper = jnp.where(r_i <= c_i, 1.0, 0.0).astype(BF16)
    lower = jnp.where(c_i <= r_i, 1.0, 0.0).astype(BF16)
    lower_pg = jnp.where(lax.broadcasted_iota(I32, (npg, npg), 1)
                         <= lax.broadcasted_iota(I32, (npg, npg), 0), 1.0, 0.0).astype(BF16)
    j_row = lax.broadcasted_iota(I32, (1, topk), 1).astype(F32)
    p_col = lax.broadcasted_iota(I32, (npg, 1), 0).astype(F32)

    def page_prefix(x):
        pin = _dot(x.astype(BF16), upper)
        rt = pin[:, pgs - 1:pgs]
        incl = _dot(lower_pg, jnp.broadcast_to(rt, (npg, pgs)).astype(BF16))[:, 0:1]
        return pin, rt, incl

    for b in range(db):
        eq_b = eq[b]
        pin_e, rt_e, incl_e = page_prefix(eq_b)
        rank_e = (incl_e - rt_e) + pin_e
        need_b = need[b]
        sel = gt[b] + jnp.where(rank_e <= need_b, eq_b, 0.0)
        n_eq = incl_e[npg - 1:npg, :]
        new_sel = gt_n[b] + jnp.where(n_eq + 1.0 <= need_b, eq_n[b], 0.0)

        pin, rt, incl = page_prefix(sel)
        ex = incl - rt
        n_sel = incl[npg - 1:npg, :]
        page_j = jnp.sum(jnp.where(incl <= j_row, 1.0, 0.0), axis=0, keepdims=True)
        oh_t = p_col == page_j
        ex_j = jnp.sum(jnp.where(oh_t, ex, 0.0), axis=0, keepdims=True)
        r_j = j_row - ex_j + 1.0
        pin_t = _dot(lower, sel.T.astype(BF16))
        rows_t = _dot(pin_t.astype(BF16), jnp.where(oh_t, 1.0, 0.0).astype(BF16))
        off_j = jnp.sum(jnp.where(rows_t < r_j, 1.0, 0.0), axis=0, keepdims=True)
        phys_j = jnp.sum(jnp.where(oh_t, ptt_ref[:, b:b + 1], 0.0), axis=0, keepdims=True)
        idx = jnp.where(j_row < n_sel, phys_j * float(pgs) + off_j, 0.0)
        idx_ref[b:b + 1, :] = idx.astype(I32)
        nv_ref[b:b + 1, :] = jnp.broadcast_to(n_sel, (1, LANES)).astype(I32)
        ns_ref[b:b + 1, :] = jnp.broadcast_to(new_sel, (1, LANES)).astype(I32)


def _dsa_sample_select(sc3, sn, pt_t, *, topk):
    db = sc3.shape[0]
    return pl.pallas_call(
        functools.partial(_dsa_sample_select_kernel, topk=topk),
        out_shape=(jax.ShapeDtypeStruct((db, topk), I32), jax.ShapeDtypeStruct((db, LANES), I32),
                   jax.ShapeDtypeStruct((db, LANES), I32)),
        in_specs=[_vmem_spec(), _vmem_spec(), _vmem_spec()],
        out_specs=(_vmem_spec(), _vmem_spec(), _vmem_spec()),
        compiler_params=_cparams(None),
        name="dsa_sample_select",
    )(sc3, sn, pt_t)


def _dsa_sample_attend_kernel(idx_ref, nv_ref, ns_ref, q_ref, kn_ref, vn_ref, ck_ref, cv_ref,
                              o_ref, kbuf, vbuf, sem, *, topk, nseq):
    b = pl.program_id(0)
    slot = b % 2
    nh = A_HEADS
    hd = q_ref.shape[2] // nh

    def row_copies(seq, j, sl):
        r = idx_ref[seq, j]
        dst = pl.ds(pl.multiple_of(j * nh, nh), nh)
        return (pltpu.make_async_copy(ck_ref.at[0, r], kbuf.at[sl, dst], sem.at[0, sl]),
                pltpu.make_async_copy(cv_ref.at[0, r], vbuf.at[sl, dst], sem.at[1, sl]))

    def issue(seq, sl):
        def body(j, carry):
            ck, cv = row_copies(seq, j, sl)
            ck.start()
            cv.start()
            return carry
        lax.fori_loop(0, topk, body, 0, unroll=8)

    @pl.when(b == 0)
    def _():
        issue(0, 0)

    @pl.when(b + 1 < nseq)
    def _():
        issue(b + 1, 1 - slot)

    def wait_body(j, carry):
        ck, cv = row_copies(b, j, slot)
        ck.wait()
        cv.wait()
        return carry
    lax.fori_loop(0, topk, wait_body, 0, unroll=8)

    n_valid = nv_ref[b]
    new_on = ns_ref[b] > 0
    j_row = lax.broadcasted_iota(I32, (1, topk), 1)
    for h in range(nh):
        hs = slice(h * hd, (h + 1) * hd)
        qh = q_ref[0, :, hs]
        kh = kbuf[slot, pl.ds(h, topk, stride=nh), :]
        vh = vbuf[slot, pl.ds(h, topk, stride=nh), :]
        s = _dot_nt(qh.astype(BF16), kh.astype(BF16))
        s = jnp.where(j_row < n_valid, s, -jnp.inf)
        s_new = jnp.sum(qh * kn_ref[0, :, hs], axis=-1, keepdims=True)
        s_new = jnp.where(new_on, s_new, -jnp.inf)
        m = jnp.maximum(jnp.max(s, axis=-1, keepdims=True), s_new)
        p = jnp.exp2(s - m)
        p_new = jnp.exp2(s_new - m)
        l = jnp.sum(p, axis=-1, keepdims=True) + p_new
        o = _dot(p.astype(BF16), vh.astype(BF16)) + p_new * vn_ref[0, :, hs]
        o_ref[0, :, hs] = o / l


def _dsa_sample_attend(idx, nv, ns, qs3, kn3, vn3, cache_k, cache_v, *, topk):
    db = idx.shape[0]
    w = qs3.shape[2]
    hd = w // A_HEADS
    row = pl.BlockSpec((1, 1, w), lambda b, *_: (b, 0, 0))
    grid_spec = pltpu.PrefetchScalarGridSpec(
        num_scalar_prefetch=3,
        grid=(db,),
        in_specs=[row, row, row, pl.BlockSpec(memory_space=pl.ANY), pl.BlockSpec(memory_space=pl.ANY)],
        out_specs=row,
        scratch_shapes=[pltpu.VMEM((2, topk * A_HEADS, hd), F32), pltpu.VMEM((2, topk * A_HEADS, hd), F32),
                        pltpu.SemaphoreType.DMA((2, 2))],
    )
    return pl.pallas_call(
        functools.partial(_dsa_sample_attend_kernel, topk=topk, nseq=db),
        out_shape=jax.ShapeDtypeStruct((db, 1, w), F32),
        grid_spec=grid_spec,
        compiler_params=_cparams(("arbitrary",)),
        name="dsa_sample_attend",
    )(idx, nv, ns, qs3, kn3, vn3, cache_k, cache_v)


def _merge_kernel(hm_ref, ha_ref, gm_ref, ga_ref, x_ref, wmo_ref, wao_ref, wout_ref, y_ref):
    t1 = _dot(hm_ref[...].astype(BF16), wmo_ref[...])
    t2 = _dot(ha_ref[...].astype(BF16), wao_ref[...])
    z = _sigmoid(gm_ref[...]) * t1 + _sigmoid(ga_ref[...]) * t2
    y_ref[...] = x_ref[...] + _dot(z.astype(BF16), wout_ref[...])


def _merge(hm, ha, p, x, wmo, wao, wout, *, tm):
    m, d = x.shape
    w = hm.shape[1]
    return pl.pallas_call(
        _merge_kernel,
        out_shape=jax.ShapeDtypeStruct((m, d), F32),
        grid=(pl.cdiv(m, tm),),
        in_specs=[
            pl.BlockSpec((tm, w), lambda i: (i, 0)),
            pl.BlockSpec((tm, w), lambda i: (i, 0)),
            pl.BlockSpec((tm, d), lambda i: (i, 0)),
            pl.BlockSpec((tm, d), lambda i: (i, 1)),
            pl.BlockSpec((tm, d), lambda i: (i, 0)),
            _vmem_spec(), _vmem_spec(), _vmem_spec(),
        ],
        out_specs=pl.BlockSpec((tm, d), lambda i: (i, 0)),
        compiler_params=_cparams(("arbitrary",)),
        name="merge",
    )(hm, ha, p, p, x, wmo, wao, wout)


FFN_TM = 512
FFN_TF = 512
PROJ_TM = 1024
MERGE_TM = 256
MLSTM_ROWS = 256
MLSTM_SAMPLE_GROUP = 8
DSA_TQ = 256
CHUNKS_PER_TRIP = 4
DSA_PREP_TM = 512


def kernel(x_prompt, x_sample, cache_k, cache_v, cache_kidx, state_conv, state_C, state_n, state_m, page_table, norm_ffn1, w_ffn1_up, w_ffn1_down, norm_mix, w_in, b_if, w_conv, b_conv, w_qk_m, norm_mlstm_out, norm_q, norm_k, w_mo, w_ao, w_out, norm_ffn2, w_ffn2_up, w_ffn2_down):
    depth = w_in.shape[0]
    assert depth == 1, "single-layer step"
    bsz, seq, d = x_prompt.shape
    assert bsz == 1
    db, dseq, _ = x_sample.shape
    assert dseq == 1
    n_pages = page_table.shape[1]
    assert cache_k.shape[2] == PAGE_SIZE
    past = n_pages * PAGE_SIZE
    topk_s = min(TOPK_MAX, (past + dseq) // 4)
    l = 0

    xp = x_prompt.reshape(seq, d)
    xs = x_sample.reshape(db, d)

    wu1, wd1 = w_ffn1_up[l].astype(BF16), w_ffn1_down[l].astype(BF16)
    wu2, wd2 = w_ffn2_up[l].astype(BF16), w_ffn2_down[l].astype(BF16)
    w_t, w_small, tab = _prep_w_in(w_in[l], d)
    wmo, wao, wout = w_mo[l].astype(BF16), w_ao[l].astype(BF16), w_out[l].astype(BF16)
    wqk = w_qk_m[l].astype(BF16)
    g1, gmix, g2 = norm_ffn1[l][None], norm_mix[l][None], norm_ffn2[l][None]
    gq, gk = norm_q[l][None], norm_k[l][None]
    gmo = norm_mlstm_out[l]
    bif_row = jnp.zeros((1, LANES), F32).at[0, S_IGATE:S_IGATE + 2 * M_HEADS].set(b_if[l])
    wcv, bcv = w_conv[l], b_conv[l][None]

    xp1, hp = _ffn(xp, g1, wu1, wd1, tm=FFN_TM, tf=FFN_TF, g_next=gmix)
    xs1, hs = _ffn(xs, g1, wu1, wd1, tm=db, tf=FFN_TF, g_next=gmix)

    pp, pps = _inproj(hp, w_t, tab, tm=PROJ_TM), _inproj_small(hp, w_small, tm=PROJ_TM)
    sp, sps = _inproj(hs, w_t, tab, tm=db), _inproj_small(hs, w_small, tm=db)

    hm_p, conv_p, c_p, n_p, m_p = _mlstm_prompt(pp, pps, bif_row, wcv, bcv, wqk, gmo, rows=MLSTM_ROWS)
    k32, v32, kb, vb, ki, kmx = _dsa_prep(pp, pps, gk, tm=DSA_PREP_TM)
    ha_p = _dsa_prompt(pp, pps, gq, kmx, ki, kb, vb, tq=DSA_TQ)
    yp = _merge(hm_p, ha_p, pp, xp1, wmo, wao, wout, tm=MERGE_TM)
    out_p = _ffn(yp, g2, wu2, wd2, tm=FFN_TM, tf=FFN_TF)

    conv_t = jnp.transpose(state_conv[l], (1, 0, 2))
    hm_s, conv_s, c_s, n_s, m_s = _mlstm_sample(sp, sps, conv_t, state_C[l], state_n[l], state_m[l],
                                                bif_row, wcv, bcv, wqk, gmo, gb=MLSTM_SAMPLE_GROUP)
    qs, kn_s, vn_s, sn = _dsa_sample_prep(sp, sps, gq, gk)
    qi3 = sp[:, C_QI * GROUP:(C_QI + 1) * GROUP].reshape(db, IDX_HEADS, IDX_DIM)
    w3 = (sps[:, S_WIDX:S_WIDX + IDX_HEADS] * ((IDX_HEADS ** -0.5) * (IDX_DIM ** -0.5))).reshape(db, IDX_HEADS, 1)
    sc = _dsa_sample_scores(page_table, qi3, w3, jnp.swapaxes(cache_kidx, 2, 3))
    sc3 = sc.reshape(db, n_pages, PAGE_SIZE)
    idx, nv, ns = _dsa_sample_select(sc3, sn, page_table.T.astype(F32), topk=topk_s)
    n_rows = cache_k.shape[1] * PAGE_SIZE
    ha_s = _dsa_sample_attend(idx, nv[:, 0], ns[:, 0], qs.reshape(db, 1, GROUP),
                              kn_s.reshape(db, 1, GROUP), vn_s.reshape(db, 1, GROUP),
                              cache_k.reshape(depth, n_rows, A_HEADS, GROUP // A_HEADS),
                              cache_v.reshape(depth, n_rows, A_HEADS, GROUP // A_HEADS), topk=topk_s)
    ys = _merge(hm_s, ha_s.reshape(db, GROUP), sp, xs1, wmo, wao, wout, tm=db)
    out_s = _ffn(ys, g2, wu2, wd2, tm=db, tf=FFN_TF)

    hd = GROUP // A_HEADS
    hv = GROUP // M_HEADS
    hk = hv // 2
    return (
        out_p.reshape(bsz, seq, d),
        out_s.reshape(db, dseq, d),
        k32.reshape(depth, bsz, seq, A_HEADS, hd),
        v32.reshape(depth, bsz, seq, A_HEADS, hd),
        pps[:, S_KIDX:S_KIDX + IDX_DIM].reshape(depth, bsz, seq, IDX_DIM),
        conv_p.reshape(depth, bsz, CONV_W - 1, GROUP),
        c_p.reshape(depth, bsz, M_HEADS, hk, hv),
        n_p[:M_HEADS].reshape(depth, bsz, M_HEADS, hk),
        m_p[:M_HEADS, 0].reshape(depth, bsz, M_HEADS),
        kn_s.reshape(depth, db, dseq, A_HEADS, hd),
        vn_s.reshape(depth, db, dseq, A_HEADS, hd),
        sps[:, S_KIDX:S_KIDX + IDX_DIM].reshape(depth, db, dseq, IDX_DIM),
        jnp.transpose(conv_s, (1, 0, 2)).reshape(depth, db, CONV_W - 1, GROUP),
        c_s.reshape(depth, db, M_HEADS, hk, hv),
        n_s.reshape(depth, db, M_HEADS, hk),
        m_s.reshape(depth, db, M_HEADS),
    )
```
